```python
import math
import jax
import jax.numpy as jnp
from jax import lax
import numpy as np

D_MODEL = 2048
BATCH = 8
SEQ = 4096
DEPTH = 4

GRID_W = 64
CTX_LEN = 256
CHUNK = 64
SHORT_CONV = 5
RMS_EPS = 1e-6

GDN_HEADS = 6
GDN_HEAD_DIM = 128
GDN_DIM = GDN_HEADS * GDN_HEAD_DIM

SSM_HEADS = 12
SSM_HEAD_DIM = 64
SSM_DIM = SSM_HEADS * SSM_HEAD_DIM
SSM_GROUPS = 2
SSM_STATE = 128
SSM_CONV_DIM = SSM_DIM + 2 * SSM_GROUPS * SSM_STATE

S5_GROUPS = 32
S5_GROUP = 16
S5_DIM = S5_GROUPS * S5_GROUP
S5_STATE = 64

MIX_DIM = GDN_DIM + SSM_DIM + S5_DIM
D_FF = ((8 * D_MODEL + 767) // 768) * 256

PROJ_SIZES = (3 * GDN_DIM, GDN_DIM, 2 * GDN_HEADS, 2 * GDN_HEADS, SSM_DIM, SSM_CONV_DIM, 2 * SSM_HEADS, S5_DIM)
PROJ_DIM = 4 * GDN_DIM + 4 * GDN_HEADS + SSM_DIM + SSM_CONV_DIM + 2 * SSM_HEADS + S5_DIM

kernel_name = "hybrid_gdn_ssd_s5_prefix_dit"


def rms_norm(x, w):
    xf = x.astype(jnp.float32)
    y = xf * lax.rsqrt(jnp.mean(jnp.square(xf), axis=-1, keepdims=True) + RMS_EPS)
    return (y * w.astype(jnp.float32)).astype(x.dtype)


def l2_normalize(t):
    return t * lax.rsqrt(jnp.sum(jnp.square(t), axis=-1, keepdims=True) + 1e-6)


def modulate(h, shift, scale):
    return h * (1 + scale) + shift


def _same(t):
    return t


def _rev(t):
    return jnp.flip(t, axis=1)


def split_proj(h):
    bounds = np.cumsum(PROJ_SIZES)[:-1].tolist()
    return jnp.split(h, bounds, axis=-1)


def short_conv(u, w, n_rows):
    b, l, ch = u.shape
    us = u.reshape(b * n_rows, l // n_rows, ch)
    y = lax.conv_general_dilated(
        us, w.astype(u.dtype)[:, None, :], window_strides=(1,),
        padding=[(SHORT_CONV // 2, SHORT_CONV // 2)],
        dimension_numbers=("NWC", "WIO", "NWC"), feature_group_count=ch)
    return y.reshape(b, l, ch)


def segsum(a):
    t = a.shape[-1]
    cs = jnp.cumsum(a, axis=-1)
    diff = cs[..., :, None] - cs[..., None, :]
    return jnp.where(jnp.tril(jnp.ones((t, t), bool)), diff, -jnp.inf)


def gdn_chunked(q, k, v, beta, g, s0):
    bsz, l, nh, dk = q.shape
    dv = v.shape[-1]
    nc = l // CHUNK

    def blocks(t):
        t = jnp.swapaxes(t, 1, 2)
        return t.reshape(t.shape[0], t.shape[1], nc, CHUNK, *t.shape[3:])

    q, k, v, beta, g = (blocks(t) for t in (q, k, v, beta, g))
    g_cum = jnp.cumsum(g, axis=-1)
    gamma = jnp.exp(segsum(g))
    strict = jnp.tril(jnp.ones((CHUNK, CHUNK), bool), -1)
    m = jnp.where(strict, jnp.einsum("bhnid,bhnjd->bhnij", k, k) * gamma * beta[..., :, None], 0.0)
    rhs = jnp.concatenate([v * beta[..., None], k * (beta * jnp.exp(g_cum))[..., None]], axis=-1)
    sol = lax.linalg.triangular_solve(m, rhs, left_side=True, lower=True, unit_diagonal=True)
    u_blk, w_blk = sol[..., :dv], sol[..., dv:]
    qk = jnp.einsum("bhnid,bhnjd->bhnij", q, k) * gamma
    q_dec = q * jnp.exp(g_cum)[..., None]
    k_dec = k * jnp.exp(g_cum[..., -1:] - g_cum)[..., None]
    g_tot = jnp.exp(g_cum[..., -1])

    def step(s, blk):
        u_i, w_i, qk_i, qd_i, kd_i, gt_i = blk
        v_new = u_i - w_i @ s
        o_i = qd_i @ s + qk_i @ v_new
        s = s * gt_i[..., None, None] + jnp.swapaxes(kd_i, -1, -2) @ v_new
        return s, o_i

    xs = tuple(jnp.moveaxis(t, 2, 0) for t in (u_blk, w_blk, qk, q_dec, k_dec, g_tot))
    s_fin, o = lax.scan(step, s0, xs)
    o = jnp.moveaxis(o, 0, 2).reshape(bsz, nh, l, dv)
    return jnp.swapaxes(o, 1, 2), s_fin


def gdn_inputs(qkv, a, b, conv_w, a_log, dt_bias, n_rows):
    bsz, l, _ = qkv.shape
    qkv = jax.nn.silu(short_conv(qkv, conv_w, n_rows)).astype(jnp.float32)
    qkv = qkv.reshape(bsz, l, 3, GDN_HEADS, GDN_HEAD_DIM)
    q = l2_normalize(qkv[:, :, 0]) * (GDN_HEAD_DIM ** -0.5)
    k = l2_normalize(qkv[:, :, 1])
    v = qkv[:, :, 2]
    a = a.astype(jnp.float32).reshape(bsz, l, 2, GDN_HEADS)
    g = -jnp.exp(a_log.astype(jnp.float32)) * jax.nn.softplus(a + dt_bias.astype(jnp.float32))
    beta = jax.nn.sigmoid(b.astype(jnp.float32).reshape(bsz, l, 2, GDN_HEADS))
    return q, k, v, beta, g


def gdn_gate_norm(o, z, w):
    bsz, l = o.shape[:2]
    zh = z.astype(jnp.float32).reshape(bsz, l, GDN_HEADS, GDN_HEAD_DIM)
    return (rms_norm(o, w) * jax.nn.silu(zh)).reshape(bsz, l, GDN_DIM)


def gdn_mixer(cols_c, cols_l, rows, conv_w, a_log, dt_bias, norm_w):
    qkv_c, z_c, a_c, b_c = cols_c
    qkv_l, z_l, a_l, b_l = cols_l
    qc, kc, vc, betac, gc = gdn_inputs(qkv_c, a_c, b_c, conv_w, a_log, dt_bias, 1)
    ql, kl, vl, betal, gl = gdn_inputs(qkv_l, a_l, b_l, conv_w, a_log, dt_bias, rows)
    s0 = jnp.zeros((qc.shape[0], GDN_HEADS, GDN_HEAD_DIM, GDN_HEAD_DIM), jnp.float32)
    o_c = jnp.zeros_like(vc)
    o_l = jnp.zeros_like(vl)
    for d, orient in enumerate((_same, _rev)):
        oc, s_ctx = gdn_chunked(orient(qc), orient(kc), orient(vc), orient(betac[:, :, d]), orient(gc[:, :, d]), s0)
        ol, _ = gdn_chunked(orient(ql), orient(kl), orient(vl), orient(betal[:, :, d]), orient(gl[:, :, d]), s_ctx)
        o_c = o_c + orient(oc)
        o_l = o_l + orient(ol)
    return gdn_gate_norm(o_c, z_c, norm_w), gdn_gate_norm(o_l, z_l, norm_w)


def ssd_chunked(x, dt, a, bm, cm, h0):
    bsz, l, nh, hp = x.shape
    ng, ns = bm.shape[-2:]
    r = nh // ng
    nc = l // CHUNK
    xdt = (x * dt[..., None]).reshape(bsz, nc, CHUNK, ng, r, hp)
    da = (dt * a).reshape(bsz, nc, CHUNK, ng, r).transpose(0, 3, 4, 1, 2)
    bc = bm.reshape(bsz, nc, CHUNK, ng, ns)
    cc = cm.reshape(bsz, nc, CHUNK, ng, ns)
    a_cum = jnp.cumsum(da, axis=-1)
    decay_in = jnp.exp(segsum(da))
    cb = jnp.einsum("bclgn,bcsgn->bgcls", cc, bc)
    y_diag = jnp.einsum("bgcls,bgrcls,bcsgrp->bclgrp", cb, decay_in, xdt)
    decay_to_end = jnp.exp(a_cum[..., -1:] - a_cum)
    states = jnp.einsum("bclgn,bgrcl,bclgrp->bcgrpn", bc, decay_to_end, xdt)
    states = jnp.concatenate([h0.reshape(bsz, 1, ng, r, hp, ns), states], axis=1)
    chunk_decay = jnp.exp(segsum(jnp.pad(a_cum[..., -1], ((0, 0), (0, 0), (0, 0), (1, 0)))))
    states = jnp.einsum("bgrzc,bcgrpn->bzgrpn", chunk_decay, states)
    prev, final = states[:, :-1], states[:, -1]
    y_off = jnp.einsum("bclgn,bcgrpn,bgrcl->bclgrp", cc, prev, jnp.exp(a_cum))
    return (y_diag + y_off).reshape(bsz, l, nh, hp), final.reshape(bsz, nh, hp, ns)


def ssd_inputs(xbc, dt, conv_w, dt_bias, n_rows):
    bsz, l, _ = xbc.shape
    xbc = jax.nn.silu(short_conv(xbc, conv_w, n_rows)).astype(jnp.float32)
    nb = SSM_GROUPS * SSM_STATE
    xs = xbc[..., :SSM_DIM].reshape(bsz, l, SSM_HEADS, SSM_HEAD_DIM)
    bm = xbc[..., SSM_DIM:SSM_DIM + nb].reshape(bsz, l, SSM_GROUPS, SSM_STATE)
    cm = xbc[..., SSM_DIM + nb:].reshape(bsz, l, SSM_GROUPS, SSM_STATE)
    dt = jax.nn.softplus(dt.astype(jnp.float32).reshape(bsz, l, 2, SSM_HEADS) + dt_bias.astype(jnp.float32))
    return xs, bm, cm, dt


def ssd_gate_norm(y, z, w):
    bsz, l = y.shape[:2]
    gated = y.reshape(bsz, l, SSM_DIM) * jax.nn.silu(z.astype(jnp.float32))
    gated = gated.reshape(bsz, l, SSM_GROUPS, SSM_DIM // SSM_GROUPS)
    return rms_norm(gated, w.reshape(SSM_GROUPS, SSM_DIM // SSM_GROUPS)).reshape(bsz, l, SSM_DIM)


def ssd_mixer(cols_c, cols_l, rows, conv_w, a_log, dt_bias, d_skip, norm_w):
    z_c, xbc_c, dt_c = cols_c
    z_l, xbc_l, dt_l = cols_l
    xc, bc, cc, dtc = ssd_inputs(xbc_c, dt_c, conv_w, dt_bias, 1)
    xl, bl, cl, dtl = ssd_inputs(xbc_l, dt_l, conv_w, dt_bias, rows)
    a = -jnp.exp(a_log.astype(jnp.float32))
    d_h = d_skip.astype(jnp.float32)[:, None]
    h0 = jnp.zeros((xc.shape[0], SSM_HEADS, SSM_HEAD_DIM, SSM_STATE), jnp.float32)
    y_c = xc * d_h
    y_l = xl * d_h
    for d, orient in enumerate((_same, _rev)):
        yc, h_ctx = ssd_chunked(orient(xc), orient(dtc[:, :, d]), a[d], orient(bc), orient(cc), h0)
        yl, _ = ssd_chunked(orient(xl), orient(dtl[:, :, d]), a[d], orient(bl), orient(cl), h_ctx)
        y_c = y_c + orient(yc)
        y_l = y_l + orient(yl)
    return ssd_gate_norm(y_c, z_c, norm_w), ssd_gate_norm(y_l, z_l, norm_w)


def _diag_combine(e1, e2):
    a1, b1 = e1
    a2, b2 = e2
    return a1 * a2, a2 * b1 + b2


def s5_scan(u, lam_bar, b_bar, x0):
    bu = jnp.einsum("blgh,gph->blgp", u.astype(jnp.complex64), b_bar)
    bu = bu.at[:, 0].add(lam_bar * x0)
    a = jnp.broadcast_to(lam_bar, bu.shape)
    _, xs = lax.associative_scan(_diag_combine, (a, bu), axis=1)
    return xs, xs[:, -1]


def s5_readout(xs, c_re, c_im):
    return (jnp.einsum("blgp,ghp->blgh", jnp.real(xs), c_re.astype(jnp.float32))
            - jnp.einsum("blgp,ghp->blgh", jnp.imag(xs), c_im.astype(jnp.float32)))


def s5_glu(y, w_glu, b_glu):
    y = jax.nn.gelu(y.reshape(y.shape[0], y.shape[1], S5_DIM))
    return y * jax.nn.sigmoid(y @ w_glu.astype(jnp.float32) + b_glu.astype(jnp.float32))


def s5_mixer(u_c, u_l, lam_re, lam_im, log_step, b_re, b_im, c_re, c_im, d_skip, w_glu, b_glu):
    def grouped(u):
        return u.astype(jnp.float32).reshape(u.shape[0], u.shape[1], S5_GROUPS, S5_GROUP)

    uc, ul = grouped(u_c), grouped(u_l)
    d_g = d_skip.astype(jnp.float32).reshape(S5_GROUPS, S5_GROUP)
    y_c = uc * d_g
    y_l = ul * d_g
    x0 = jnp.zeros((uc.shape[0], S5_GROUPS, S5_STATE), jnp.complex64)
    for d, orient in enumerate((_same, _rev)):
        lam = lax.complex(lam_re[d].astype(jnp.float32), lam_im[d].astype(jnp.float32))
        step = jnp.exp(log_step[d].astype(jnp.float32))[:, None]
        lam_bar = jnp.exp(lam * step)
        b_mat = lax.complex(b_re[d].astype(jnp.float32), b_im[d].astype(jnp.float32))
        b_bar = ((lam_bar - 1.0) / lam)[..., None] * b_mat
        xc, x_ctx = s5_scan(orient(uc), lam_bar, b_bar, x0)
        xl, _ = s5_scan(orient(ul), lam_bar, b_bar, x_ctx)
        y_c = y_c + orient(s5_readout(xc, c_re[d], c_im[d]))
        y_l = y_l + orient(s5_readout(xl, c_re[d], c_im[d]))
    return s5_glu(y_c, w_glu, b_glu), s5_glu(y_l, w_glu, b_glu)


def swiglu(h, w_in, w_out):
    gate, up = jnp.split(h @ w_in, 2, axis=-1)
    return (jax.nn.silu(gate) * up) @ w_out


def concat_groups(parts, dtype):
    return jnp.concatenate([p.astype(dtype) for p in parts], axis=-1)


def setup_inputs(seed: int = 0) -> dict:
    key = jax.random.key(seed)
    k = jax.random.split(key, 32)
    f32 = jnp.float32
    L = DEPTH

    def normal(kk, shape, scale):
        return scale * jax.random.normal(kk, shape, f32)

    def dt_bias_init(kk, shape):
        dt = jnp.exp(jax.random.uniform(kk, shape, f32, math.log(1e-3), math.log(1e-1)))
        return dt + jnp.log(-jnp.expm1(-dt))

    s5_shape = (L, 2, S5_GROUPS, S5_STATE)
    return {
        "x": normal(k[0], (BATCH, SEQ, D_MODEL), 1.0),
        "c": normal(k[1], (BATCH, D_MODEL), 1.0),
        "ctx": normal(k[2], (BATCH, CTX_LEN, D_MODEL), 1.0),
        "c_ctx": normal(k[3], (D_MODEL,), 1.0),
        "w_mod": normal(k[4], (L, D_MODEL, 6 * D_MODEL), 0.5 * D_MODEL ** -0.5),
        "b_mod": normal(k[5], (L, 6 * D_MODEL), 0.02),
        "norm_w": 1.0 + normal(k[6], (L, 4, D_MODEL), 0.05),
        "w_in": normal(k[7], (L, D_MODEL, PROJ_DIM), D_MODEL ** -0.5),
        "w_out": normal(k[8], (L, MIX_DIM, D_MODEL), MIX_DIM ** -0.5),
        "gdn_conv": normal(k[9], (L, SHORT_CONV, 3 * GDN_DIM), SHORT_CONV ** -0.5),
        "gdn_a_log": jnp.log(jax.random.uniform(k[10], (L, 2, GDN_HEADS), f32, 1.0, 16.0)),
        "gdn_dt_bias": dt_bias_init(k[11], (L, 2, GDN_HEADS)),
        "gdn_norm": 1.0 + normal(k[12], (L, GDN_HEAD_DIM), 0.05),
        "ssm_conv": normal(k[13], (L, SHORT_CONV, SSM_CONV_DIM), SHORT_CONV ** -0.5),
        "ssm_a_log": jnp.log(jax.random.uniform(k[14], (L, 2, SSM_HEADS), f32, 1.0, 16.0)),
        "ssm_dt_bias": dt_bias_init(k[15], (L, 2, SSM_HEADS)),
        "ssm_d": 1.0 + normal(k[16], (L, SSM_HEADS), 0.1),
        "ssm_norm": 1.0 + normal(k[17], (L, SSM_DIM), 0.05),
        "s5_lam_re": -0.5 + normal(k[18], s5_shape, 0.01),
        "s5_lam_im": np.pi * jnp.arange(S5_STATE, dtype=f32) + normal(k[19], s5_shape, 0.01),
        "s5_log_step": jax.random.uniform(k[20], (L, 2, S5_GROUPS), f32, math.log(1e-3), math.log(1e-1)),
        "s5_b_re": normal(k[21], (L, 2, S5_GROUPS, S5_STATE, S5_GROUP), (2 * S5_GROUP) ** -0.5),
        "s5_b_im": normal(k[22], (L, 2, S5_GROUPS, S5_STATE, S5_GROUP), (2 * S5_GROUP) ** -0.5),
        "s5_c_re": normal(k[23], (L, 2, S5_GROUPS, S5_GROUP, S5_STATE), (2 * S5_STATE) ** -0.5),
        "s5_c_im": normal(k[24], (L, 2, S5_GROUPS, S5_GROUP, S5_STATE), (2 * S5_STATE) ** -0.5),
        "s5_d": normal(k[25], (L, S5_DIM), 1.0),
        "s5_w_glu": normal(k[26], (L, S5_DIM, S5_DIM), S5_DIM ** -0.5),
        "s5_b_glu": normal(k[27], (L, S5_DIM), 0.02),
        "w_ffn_in": normal(k[28], (L, D_MODEL, 2 * D_FF), D_MODEL ** -0.5),
        "w_ffn_out": normal(k[29], (L, D_FF, D_MODEL), D_FF ** -0.5),
    }


def reference(x, c, ctx, c_ctx, w_mod, b_mod, norm_w, w_in, w_out, gdn_conv, gdn_a_log, gdn_dt_bias,
              gdn_norm, ssm_conv, ssm_a_log, ssm_dt_bias, ssm_d, ssm_norm, s5_lam_re, s5_lam_im,
              s5_log_step, s5_b_re, s5_b_im, s5_c_re, s5_c_im, s5_d, s5_w_glu, s5_b_glu,
              w_ffn_in, w_ffn_out):
    rows = x.shape[1] // GRID_W
    act_c = jax.nn.silu(c)
    act_cc = jax.nn.silu(c_ctx)
    for i in range(DEPTH):
        mod_l = jnp.split((act_c @ w_mod[i] + b_mod[i])[:, None, :], 6, axis=-1)
        mod_c = jnp.split(act_cc @ w_mod[i] + b_mod[i], 6, axis=-1)
        update_ctx = i < DEPTH - 1

        h_l = modulate(rms_norm(x, norm_w[i, 0]), mod_l[0], mod_l[1]) @ w_in[i]
        h_c = modulate(rms_norm(ctx, norm_w[i, 0]), mod_c[0], mod_c[1]) @ w_in[i]
        qkv_l, gz_l, ga_l, gb_l, sz_l, xbc_l, sdt_l, u_l = split_proj(h_l)
        qkv_c, gz_c, ga_c, gb_c, sz_c, xbc_c, sdt_c, u_c = split_proj(h_c)
        gdn_c, gdn_l = gdn_mixer((qkv_c, gz_c, ga_c, gb_c), (qkv_l, gz_l, ga_l, gb_l), rows,
                                 gdn_conv[i], gdn_a_log[i], gdn_dt_bias[i], gdn_norm[i])
        ssd_c, ssd_l = ssd_mixer((sz_c, xbc_c, sdt_c), (sz_l, xbc_l, sdt_l), rows,
                                 ssm_conv[i], ssm_a_log[i], ssm_dt_bias[i], ssm_d[i], ssm_norm[i])
        s5o_c, s5o_l = s5_mixer(u_c, u_l, s5_lam_re[i], s5_lam_im[i], s5_log_step[i], s5_b_re[i],
                                s5_b_im[i], s5_c_re[i], s5_c_im[i], s5_d[i], s5_w_glu[i], s5_b_glu[i])
        y_l = concat_groups((gdn_l, ssd_l, s5o_l), x.dtype) @ w_out[i]
        x = x + mod_l[2] * rms_norm(y_l, norm_w[i, 1])
        if update_ctx:
            y_c = concat_groups((gdn_c, ssd_c, s5o_c), ctx.dtype) @ w_out[i]
            ctx = ctx + mod_c[2] * rms_norm(y_c, norm_w[i, 1])

        f_l = swiglu(modulate(rms_norm(x, norm_w[i, 2]), mod_l[3], mod_l[4]), w_ffn_in[i], w_ffn_out[i])
        x = x + mod_l[5] * rms_norm(f_l, norm_w[i, 3])
        if update_ctx:
            f_c = swiglu(modulate(rms_norm(ctx, norm_w[i, 2]), mod_c[3], mod_c[4]), w_ffn_in[i], w_ffn_out[i])
            ctx = ctx + mod_c[5] * rms_norm(f_c, norm_w[i, 3])
    return x
```

```python
import functools
import math

import jax
import jax.numpy as jnp
from jax import lax
from jax.experimental import pallas as pl
from jax.experimental.pallas import tpu as pltpu

F32 = jnp.float32
BF16 = jnp.bfloat16
HIGHEST = lax.Precision.HIGHEST

RMS_EPS = 1e-6
GRID_W = 64
SHORT_CONV = 5

GDN_HEADS = 6
GDN_HEAD_DIM = 128
GDN_DIM = GDN_HEADS * GDN_HEAD_DIM
SSM_HEADS = 12
SSM_HEAD_DIM = 64
SSM_DIM = SSM_HEADS * SSM_HEAD_DIM
SSM_GROUPS = 2
SSM_STATE = 128
SSM_CONV_DIM = SSM_DIM + 2 * SSM_GROUPS * SSM_STATE
SSM_GROUP_HEADS = SSM_HEADS // SSM_GROUPS
SSM_GROUP_DIM = SSM_DIM // SSM_GROUPS
S5_GROUPS = 32
S5_GROUP = 16
S5_DIM = S5_GROUPS * S5_GROUP
S5_STATE = 64
MIX_DIM = GDN_DIM + SSM_DIM + S5_DIM

COL_QKV = 0
COL_GZ = 3 * GDN_DIM
COL_SZ = COL_GZ + GDN_DIM
COL_XBC = COL_SZ + SSM_DIM
COL_U = COL_XBC + SSM_CONV_DIM
MAIN_DIM = COL_U + S5_DIM
GATE_DIM = 128
GATE_A = 0
GATE_B = 2 * GDN_HEADS
GATE_DT = 4 * GDN_HEADS

LANES = 128
ROW_BLOCK = 256
MIX_CHUNK = 128
S5_CHUNK = 16
VMEM_LIMIT = 56 * 1024 * 1024
NEG_BIG = -1e30


def _sigmoid(v):
    return 1.0 / (1.0 + jnp.exp(-v))


def _silu(v):
    return v * _sigmoid(v)


def _softplus(v):
    return jnp.maximum(v, 0.0) + jnp.log(1.0 + jnp.exp(-jnp.abs(v)))


def _rms(v, w):
    return v * lax.rsqrt(jnp.mean(v * v, axis=-1, keepdims=True) + RMS_EPS) * w


def _bdot(a, b):
    return jnp.dot(a.astype(BF16), b.astype(BF16), preferred_element_type=F32)


def _fdot(a, b):
    return jnp.dot(a, b, precision=HIGHEST, preferred_element_type=F32)


def _params(*sem):
    return pltpu.CompilerParams(dimension_semantics=sem, vmem_limit_bytes=VMEM_LIMIT)


def _mod_kernel(c_ref, w_ref, b_ref, o_ref):
    o_ref[0] = _fdot(_silu(c_ref[...]), w_ref[0]) + b_ref[0]


def _mod_call(cvec, w_mod, b_mod):
    depth, d_model, n = w_mod.shape
    rows = cvec.shape[0]
    tn = 1024
    return pl.pallas_call(
        _mod_kernel,
        out_shape=jax.ShapeDtypeStruct((depth, rows, n), F32),
        grid=(depth, n // tn),
        in_specs=[
            pl.BlockSpec((rows, d_model), lambda l, j: (0, 0)),
            pl.BlockSpec((1, d_model, tn), lambda l, j: (l, 0, j)),
            pl.BlockSpec((1, 1, tn), lambda l, j: (l, 0, j)),
        ],
        out_specs=pl.BlockSpec((1, rows, tn), lambda l, j: (l, 0, j)),
        compiler_params=_params("parallel", "parallel"),
        name="adaln_mod",
    )(cvec, w_mod, b_mod.reshape(depth, 1, n))


def _inproj_kernel(x_ref, nw_ref, sh_ref, sc_ref, w_ref, wg_ref, wgt_ref,
                   h_ref, hg_ref, hgt_ref, xn_ref, *, nb):
    @pl.when(pl.program_id(1) == 0)
    def _():
        y = _rms(x_ref[...], nw_ref[...])
        for s in range(nb):
            r = slice(s * ROW_BLOCK, (s + 1) * ROW_BLOCK)
            xn_ref[r, :] = (y[r] * (1.0 + sc_ref[s]) + sh_ref[s]).astype(BF16)
        xn = xn_ref[...]
        hg_ref[...] = jnp.dot(xn, wg_ref[...], preferred_element_type=F32)
        hgt_ref[...] = lax.dot_general(wgt_ref[...], xn, (((1,), (1,)), ((), ())),
                                       preferred_element_type=F32)

    h_ref[...] = jnp.dot(xn_ref[...], w_ref[...], preferred_element_type=F32)


def _inproj_call(x, nw, modtab, w_main, w_gate, w_gate_t, tm, tn):
    t, d = x.shape
    n = w_main.shape[1]
    nb = tm // ROW_BLOCK
    return pl.pallas_call(
        functools.partial(_inproj_kernel, nb=nb),
        out_shape=(jax.ShapeDtypeStruct((t, n), F32),
                   jax.ShapeDtypeStruct((t, GATE_DIM), F32),
                   jax.ShapeDtypeStruct((GATE_DIM, t), F32)),
        grid=(t // tm, n // tn),
        in_specs=[
            pl.BlockSpec((tm, d), lambda i, j: (i, 0)),
            pl.BlockSpec((1, d), lambda i, j: (0, 0)),
            pl.BlockSpec((nb, 1, d), lambda i, j: (i, 0, 0)),
            pl.BlockSpec((nb, 1, d), lambda i, j: (i, 0, 1)),
            pl.BlockSpec((d, tn), lambda i, j: (0, j)),
            pl.BlockSpec((d, GATE_DIM), lambda i, j: (0, 0)),
            pl.BlockSpec((GATE_DIM, d), lambda i, j: (0, 0)),
        ],
        out_specs=(pl.BlockSpec((tm, tn), lambda i, j: (i, j)),
                   pl.BlockSpec((tm, GATE_DIM), lambda i, j: (i, 0)),
                   pl.BlockSpec((GATE_DIM, tm), lambda i, j: (0, i))),
        scratch_shapes=[pltpu.VMEM((tm, d), BF16)],
        compiler_params=_params("parallel", "arbitrary"),
        name="in_proj",
    )(x, nw, modtab, modtab, w_main, w_gate, w_gate_t)


def _prep_kernel(h_ref, w_ref, o_ref, *, n_q, n_k, q_scale, blocks_per_seq, ctx_blocks):
    i = pl.program_id(0)
    j = pl.program_id(1)
    x = h_ref[...]
    width = x.shape[1]
    t = lax.broadcasted_iota(jnp.int32, (ROW_BLOCK, 1), 0)
    is_ctx = (i % blocks_per_seq) < ctx_blocks
    seg = jnp.where(is_ctx, ROW_BLOCK, GRID_W)
    pos = jnp.bitwise_and(t, seg - 1)
    half = SHORT_CONV // 2
    acc = x * w_ref[half:half + 1, :]
    for dlt in range(-half, half + 1):
        if dlt == 0:
            continue
        shifted = pltpu.roll(x, (-dlt) % ROW_BLOCK, axis=0)
        ok = jnp.logical_and(pos + dlt >= 0, pos + dlt < seg)
        acc = acc + jnp.where(ok, shifted, 0.0) * w_ref[dlt + half:dlt + half + 1, :]
    y = _silu(acc)
    norm_scale = jnp.where(j < n_q, q_scale, 1.0)
    use_norm = j < n_q + n_k
    for b in range(width // LANES):
        c = slice(b * LANES, (b + 1) * LANES)
        yb = y[:, c]
        inv = lax.rsqrt(jnp.sum(yb * yb, axis=-1, keepdims=True) + 1e-6) * norm_scale
        o_ref[:, c] = yb * jnp.where(use_norm, inv, 1.0)


def _prep_call(h, col0, width, n_col_blocks, conv_w, n_q, n_k, q_scale, blocks_per_seq, ctx_blocks):
    t = h.shape[0]
    cw = jnp.zeros((8, conv_w.shape[1]), F32).at[:SHORT_CONV].set(conv_w)
    cb0 = col0 // width
    return pl.pallas_call(
        functools.partial(_prep_kernel, n_q=n_q, n_k=n_k, q_scale=q_scale,
                          blocks_per_seq=blocks_per_seq, ctx_blocks=ctx_blocks),
        out_shape=jax.ShapeDtypeStruct((t, width * n_col_blocks), F32),
        grid=(t // ROW_BLOCK, n_col_blocks),
        in_specs=[
            pl.BlockSpec((ROW_BLOCK, width), lambda i, j: (i, cb0 + j)),
            pl.BlockSpec((8, width), lambda i, j: (0, j)),
        ],
        out_specs=pl.BlockSpec((ROW_BLOCK, width), lambda i, j: (i, j)),
        compiler_params=_params("parallel", "parallel"),
        name="conv_prep",
    )(h, cw)


def _bwd_chunk(s, n_chunks, n_ctx_chunks):
    return jnp.where(s < n_ctx_chunks, n_ctx_chunks - 1 - s, n_chunks + n_ctx_chunks - 1 - s)


def _tri_masks(c):
    ii = lax.broadcasted_iota(jnp.int32, (c, c), 0)
    jj = lax.broadcasted_iota(jnp.int32, (c, c), 1)
    return ii, jj


def _gdn_kernel(q_ref, k_ref, v_ref, z_ref, gc_ref, gr_ref, hp_ref, hpc_ref, nw_ref, o_ref,
                kt_ref, s_ref, *, n_chunks, n_ctx_chunks):
    c_len = MIX_CHUNK
    ii, jj = _tri_masks(c_len)
    eye = (ii == jj).astype(F32)
    lower = (ii >= jj).astype(F32)
    upper = (ii <= jj).astype(F32)

    def transpose_k(c, carry):
        rows = pl.ds(pl.multiple_of(c * c_len, c_len), c_len)
        kt_ref[c] = k_ref[rows, :].T
        return carry

    lax.fori_loop(0, n_chunks, transpose_k, 0)
    s_ref[...] = jnp.zeros(s_ref.shape, F32)
    o_ref[...] = jnp.zeros(o_ref.shape, F32)

    neg_a_row = -jnp.exp(hp_ref[0, 0:2, :])
    bias_row = hp_ref[0, 2:4, :]
    neg_a_col = -jnp.exp(hpc_ref[0, 0:1, 0:2])
    bias_col = hpc_ref[0, 1:2, 0:2]

    def chunk(d, c):
        rows = pl.ds(pl.multiple_of(c * c_len, c_len), c_len)
        qc = q_ref[rows, :]
        kc = k_ref[rows, :]
        vc = v_ref[rows, :]
        ktc = kt_ref[c]
        gcr = gc_ref[0, rows, :]
        g_col = (neg_a_col * _softplus(gcr[:, 0:2] + bias_col))[:, d:d + 1]
        beta = _sigmoid(gcr[:, 2 + d:3 + d])
        grr = gr_ref[0, c]
        g_row = neg_a_row[d:d + 1] * _softplus(grr[d:d + 1, :] + bias_row[d:d + 1])
        fwd = d == 0
        tri = lower if fwd else upper
        tri_t = upper if fwd else lower
        cum_col = _fdot(tri, jnp.broadcast_to(g_col, (c_len, LANES)))
        cum_row = _fdot(jnp.broadcast_to(g_row, (8, c_len)), tri_t)[0:1, :]
        incl = (ii >= jj) if fwd else (ii <= jj)
        strict = (ii > jj) if fwd else (ii < jj)
        gamma = jnp.exp(jnp.where(incl, cum_col - cum_row, NEG_BIG))
        m = jnp.where(strict, _bdot(kc, ktc) * gamma, 0.0) * beta
        a = m
        inv = eye - a
        for _ in range(int(math.log2(c_len)) - 1):
            a = _fdot(a, a)
            inv = inv + _fdot(inv, a)
        ecum = jnp.exp(cum_col)
        rhs = jnp.concatenate([vc * beta, kc * (beta * ecum)], axis=1)
        sol = _fdot(inv, rhs)
        u = sol[:, :GDN_HEAD_DIM]
        w = sol[:, GDN_HEAD_DIM:]
        qk = _bdot(qc, ktc) * gamma
        qd = qc * ecum
        cend = cum_row[:, c_len - 1:c_len] if fwd else cum_row[:, 0:1]
        kdt = ktc * jnp.exp(cend - cum_row)
        state = s_ref[d]
        v_new = u - _bdot(w, state)
        o = _bdot(qd, state) + _bdot(qk, v_new)
        s_ref[d] = state * jnp.exp(cend) + _bdot(kdt, v_new)
        o_ref[rows, :] += o

    def step(s, carry):
        chunk(0, s)
        chunk(1, _bwd_chunk(s, n_chunks, n_ctx_chunks))
        return carry

    lax.fori_loop(0, n_chunks, step, 0)

    def finish(c, carry):
        rows = pl.ds(pl.multiple_of(c * c_len, c_len), c_len)
        o_ref[rows, :] = _rms(o_ref[rows, :], nw_ref[...]) * _silu(z_ref[rows, :])
        return carry

    lax.fori_loop(0, n_chunks, finish, 0)


def _gdn_call(qkv, h, gcol, grow, hp, hpc, nw, batch, lc, n_ctx_chunks):
    t = qkv.shape[0]
    n_chunks = lc // MIX_CHUNK
    hd = GDN_HEAD_DIM
    z0 = COL_GZ // hd
    return pl.pallas_call(
        functools.partial(_gdn_kernel, n_chunks=n_chunks, n_ctx_chunks=n_ctx_chunks),
        out_shape=jax.ShapeDtypeStruct((t, GDN_DIM), F32),
        grid=(batch, GDN_HEADS),
        in_specs=[
            pl.BlockSpec((lc, hd), lambda b, hh: (b, hh)),
            pl.BlockSpec((lc, hd), lambda b, hh: (b, GDN_HEADS + hh)),
            pl.BlockSpec((lc, hd), lambda b, hh: (b, 2 * GDN_HEADS + hh)),
            pl.BlockSpec((lc, hd), lambda b, hh: (b, z0 + hh)),
            pl.BlockSpec((1, lc, 8), lambda b, hh: (hh, b, 0)),
            pl.BlockSpec((1, n_chunks, 8, MIX_CHUNK), lambda b, hh: (hh, b, 0, 0)),
            pl.BlockSpec((1, 8, LANES), lambda b, hh: (hh, 0, 0)),
            pl.BlockSpec((1, 8, 8), lambda b, hh: (hh, 0, 0)),
            pl.BlockSpec((1, hd), lambda b, hh: (0, 0)),
        ],
        out_specs=pl.BlockSpec((lc, hd), lambda b, hh: (b, hh)),
        scratch_shapes=[pltpu.VMEM((n_chunks, hd, MIX_CHUNK), F32),
                        pltpu.VMEM((2, hd, hd), F32)],
        compiler_params=_params("parallel", "parallel"),
        name="gdn_mixer",
    )(qkv, qkv, qkv, h, gcol, grow, hp, hpc, nw)


def _ssd_kernel(x_ref, b_ref, c_ref, z_ref, dc_ref, dr_ref, pr_ref, pc_ref, dsk_ref, nw_ref, o_ref,
                bt_ref, cb_ref, st_ref, *, n_chunks, n_ctx_chunks):
    c_len = MIX_CHUNK
    hg = SSM_GROUP_HEADS
    hp = SSM_HEAD_DIM
    ii, jj = _tri_masks(c_len)
    lower = (ii >= jj).astype(F32)
    upper = (ii <= jj).astype(F32)

    def prepare(c, carry):
        rows = pl.ds(pl.multiple_of(c * c_len, c_len), c_len)
        btc = b_ref[rows, :].T
        bt_ref[c] = btc
        cb_ref[c] = _bdot(c_ref[rows, :], btc)
        return carry

    lax.fori_loop(0, n_chunks, prepare, 0)
    st_ref[...] = jnp.zeros(st_ref.shape, F32)
    o_ref[...] = jnp.zeros(o_ref.shape, F32)

    neg_a_row = -jnp.exp(pr_ref[0, 0:1, :])
    bias_row = pr_ref[0, 1:2, :]
    neg_a_col = -jnp.exp(pc_ref[0, :, 0:1])
    bias_col = pc_ref[0, :, 1:2]

    def chunk(d, c):
        rows = pl.ds(pl.multiple_of(c * c_len, c_len), c_len)
        fwd = d == 0
        xc = x_ref[rows, :]
        dtc = _softplus(dc_ref[0, rows, :] + bias_row)
        dtr = _softplus(dr_ref[0, c] + bias_col)
        tri = lower if fwd else upper
        tri_t = upper if fwd else lower
        cum_col = _fdot(tri, dtc * neg_a_row)
        cum_row = _fdot(dtr * neg_a_col, tri_t)
        incl = (ii >= jj) if fwd else (ii <= jj)
        cb = cb_ref[c]
        state = st_ref[d]
        y_off = _bdot(c_ref[rows, :], state)
        cend = cum_col[c_len - 1:c_len, :] if fwd else cum_col[0:1, :]
        to_end = jnp.exp(cend - cum_col)
        ecum = jnp.exp(cum_col)
        gtot = jnp.exp(cend)
        ys, xds, gts = [], [], []
        for hh in range(hg):
            col = d * hg + hh
            lanes = slice(hh * hp, (hh + 1) * hp)
            decay = jnp.exp(jnp.where(incl, cum_col[:, col:col + 1] - cum_row[col:col + 1, :], NEG_BIG))
            xdt = xc[:, lanes] * dtc[:, col:col + 1]
            ys.append(_bdot(cb * decay, xdt) + ecum[:, col:col + 1] * y_off[:, lanes])
            xds.append(xdt * to_end[:, col:col + 1])
            gts.append(jnp.broadcast_to(gtot[:, col:col + 1], (1, hp)))
        st_ref[d] = state * jnp.concatenate(gts, axis=1) + _bdot(bt_ref[c], jnp.concatenate(xds, axis=1))
        o_ref[rows, :] += jnp.concatenate(ys, axis=1)

    def step(s, carry):
        chunk(0, s)
        chunk(1, _bwd_chunk(s, n_chunks, n_ctx_chunks))
        return carry

    lax.fori_loop(0, n_chunks, step, 0)

    def finish(c, carry):
        rows = pl.ds(pl.multiple_of(c * c_len, c_len), c_len)
        y = o_ref[rows, :] + x_ref[rows, :] * dsk_ref[0]
        o_ref[rows, :] = _rms(y * _silu(z_ref[rows, :]), nw_ref[0])
        return carry

    lax.fori_loop(0, n_chunks, finish, 0)


def _ssd_call(xbc, h, dcol, drow, prow, pcol, dskip, nw, batch, lc, n_ctx_chunks):
    t = xbc.shape[0]
    n_chunks = lc // MIX_CHUNK
    gd = SSM_GROUP_DIM
    ns = SSM_STATE
    b0 = SSM_DIM // ns
    c0 = b0 + SSM_GROUPS
    z0 = COL_SZ // gd
    return pl.pallas_call(
        functools.partial(_ssd_kernel, n_chunks=n_chunks, n_ctx_chunks=n_ctx_chunks),
        out_shape=jax.ShapeDtypeStruct((t, SSM_DIM), F32),
        grid=(batch, SSM_GROUPS),
        in_specs=[
            pl.BlockSpec((lc, gd), lambda b, g: (b, g), pipeline_mode=pl.Buffered(1)),
            pl.BlockSpec((lc, ns), lambda b, g: (b, b0 + g), pipeline_mode=pl.Buffered(1)),
            pl.BlockSpec((lc, ns), lambda b, g: (b, c0 + g), pipeline_mode=pl.Buffered(1)),
            pl.BlockSpec((lc, gd), lambda b, g: (b, z0 + g), pipeline_mode=pl.Buffered(1)),
            pl.BlockSpec((1, lc, 16), lambda b, g: (g, b, 0), pipeline_mode=pl.Buffered(1)),
            pl.BlockSpec((1, n_chunks, 16, MIX_CHUNK), lambda b, g: (g, b, 0, 0)),
            pl.BlockSpec((1, 8, 16), lambda b, g: (g, 0, 0)),
            pl.BlockSpec((1, 16, 8), lambda b, g: (g, 0, 0)),
            pl.BlockSpec((1, 1, gd), lambda b, g: (g, 0, 0)),
            pl.BlockSpec((1, 1, gd), lambda b, g: (g, 0, 0)),
        ],
        out_specs=pl.BlockSpec((lc, gd), lambda b, g: (b, g)),
        scratch_shapes=[pltpu.VMEM((n_chunks, ns, MIX_CHUNK), F32),
                        pltpu.VMEM((n_chunks, MIX_CHUNK, MIX_CHUNK), F32),
                        pltpu.VMEM((2, ns, gd), F32)],
        compiler_params=_params("parallel", "parallel"),
        name="ssd_mixer",
    )(xbc, xbc, xbc, h, dcol, drow, prow, pcol, dskip, nw)


def _s5_kernel(u_ref, wso_ref, win_ref, wsi_ref, lam_ref, y_ref, ere_ref, eim_ref, xre_ref, xim_ref, *,
               n_groups, n_chunks, n_ctx_chunks):
    half = 2 * S5_STATE
    for g in range(n_groups):
        e = jnp.dot(u_ref[0, g], wso_ref[g], preferred_element_type=F32)
        ere_ref[pl.ds(g, n_chunks, stride=n_groups), :] = e[:, :half]
        eim_ref[pl.ds(g, n_chunks, stride=n_groups), :] = e[:, half:]

    lam_re = lam_ref[:, :half]
    lam_im = lam_ref[:, half:]
    is_fwd = lax.broadcasted_iota(jnp.int32, (n_groups, half), 1) < S5_STATE

    def step(s, carry):
        x_re, x_im = carry
        cf = pl.multiple_of(s * n_groups, n_groups)
        cb = pl.multiple_of(_bwd_chunk(s, n_chunks, n_ctx_chunks) * n_groups, n_groups)
        rows_f = pl.ds(cf, n_groups)
        rows_b = pl.ds(cb, n_groups)
        xre_ref[rows_f, :S5_STATE] = x_re[:, :S5_STATE]
        xim_ref[rows_f, :S5_STATE] = x_im[:, :S5_STATE]
        xre_ref[rows_b, S5_STATE:] = x_re[:, S5_STATE:]
        xim_ref[rows_b, S5_STATE:] = x_im[:, S5_STATE:]
        e_re = jnp.where(is_fwd, ere_ref[rows_f, :], ere_ref[rows_b, :])
        e_im = jnp.where(is_fwd, eim_ref[rows_f, :], eim_ref[rows_b, :])
        return (lam_re * x_re - lam_im * x_im + e_re, lam_re * x_im + lam_im * x_re + e_im)

    zero = jnp.zeros((n_groups, half), F32)
    lax.fori_loop(0, n_chunks, step, (zero, zero))

    for g in range(n_groups):
        xin = jnp.concatenate([xre_ref[pl.ds(g, n_chunks, stride=n_groups), :],
                               xim_ref[pl.ds(g, n_chunks, stride=n_groups), :]], axis=1)
        y_ref[0, g] = (jnp.dot(u_ref[0, g], win_ref[g], preferred_element_type=F32)
                       + _bdot(xin, wsi_ref[g]))


def _s5_call(u_t, w_so, w_in, w_si, lam_c, n_ctx_chunks, groups_per_step):
    batch, n_groups, n_chunks, width = u_t.shape
    gs = groups_per_step
    return pl.pallas_call(
        functools.partial(_s5_kernel, n_groups=gs, n_chunks=n_chunks, n_ctx_chunks=n_ctx_chunks),
        out_shape=jax.ShapeDtypeStruct(u_t.shape, F32),
        grid=(batch, n_groups // gs),
        in_specs=[
            pl.BlockSpec((1, gs, n_chunks, width), lambda b, g: (b, g, 0, 0)),
            pl.BlockSpec((gs, width, width), lambda b, g: (g, 0, 0)),
            pl.BlockSpec((gs, width, width), lambda b, g: (g, 0, 0)),
            pl.BlockSpec((gs, width, width), lambda b, g: (g, 0, 0)),
            pl.BlockSpec((gs, width), lambda b, g: (g, 0)),
        ],
        out_specs=pl.BlockSpec((1, gs, n_chunks, width), lambda b, g: (b, g, 0, 0)),
        scratch_shapes=[pltpu.VMEM((n_chunks * gs, width // 2), F32) for _ in range(4)],
        compiler_params=_params("parallel", "parallel"),
        name="s5_mixer",
    )(u_t, w_so, w_in, w_si, lam_c)


def _s5_operators(lam_re, lam_im, log_step, b_re, b_im, c_re, c_im):
    cs = S5_CHUNK
    step = jnp.exp(log_step)[..., None]
    mag = jnp.exp(lam_re * step)
    lb_re = mag * jnp.cos(lam_im * step)
    lb_im = mag * jnp.sin(lam_im * step)
    den = lam_re * lam_re + lam_im * lam_im
    f_re = ((lb_re - 1.0) * lam_re + lb_im * lam_im) / den
    f_im = (lb_im * lam_re - (lb_re - 1.0) * lam_im) / den
    bb_re = f_re[..., None] * b_re - f_im[..., None] * b_im
    bb_im = f_re[..., None] * b_im + f_im[..., None] * b_re
    p_re, p_im = [jnp.ones_like(lb_re)], [jnp.zeros_like(lb_re)]
    for _ in range(cs):
        p_re, p_im = (p_re + [p_re[-1] * lb_re - p_im[-1] * lb_im],
                      p_im + [p_re[-1] * lb_im + p_im[-1] * lb_re])
    p_re = jnp.stack(p_re)
    p_im = jnp.stack(p_im)
    cp_re = c_re[None] * p_re[:, :, :, None, :] - c_im[None] * p_im[:, :, :, None, :]
    cp_im = c_re[None] * p_im[:, :, :, None, :] + c_im[None] * p_re[:, :, :, None, :]
    kern = (jnp.einsum("tdgop,dgpa->tdgoa", cp_re, bb_re, precision=HIGHEST)
            - jnp.einsum("tdgop,dgpa->tdgoa", cp_im, bb_im, precision=HIGHEST))
    pos = jnp.arange(cs)
    lag = pos[None, :] - pos[:, None]
    k_f = jnp.where((lag >= 0)[:, :, None, None, None], kern[jnp.clip(lag, 0, cs), 0], 0.0)
    k_b = jnp.where((lag <= 0)[:, :, None, None, None], kern[jnp.clip(-lag, 0, cs), 1], 0.0)
    n_g = lam_re.shape[1]
    w_in = jnp.transpose(k_f + k_b, (2, 0, 4, 1, 3)).reshape(n_g, cs * S5_GROUP, cs * S5_GROUP)

    def state_out(power_idx, d):
        pr = p_re[power_idx, d][..., None]
        pi = p_im[power_idx, d][..., None]
        re = pr * bb_re[d][None] - pi * bb_im[d][None]
        im = pr * bb_im[d][None] + pi * bb_re[d][None]
        shape = (n_g, cs * S5_GROUP, S5_STATE)
        return (jnp.transpose(re, (1, 0, 3, 2)).reshape(shape),
                jnp.transpose(im, (1, 0, 3, 2)).reshape(shape))

    of_re, of_im = state_out(cs - 1 - pos, 0)
    ob_re, ob_im = state_out(pos, 1)
    w_so = jnp.concatenate([of_re, ob_re, of_im, ob_im], axis=2)

    def state_in(power_idx, d):
        re = jnp.transpose(cp_re[power_idx, d], (1, 3, 0, 2))
        im = jnp.transpose(cp_im[power_idx, d], (1, 3, 0, 2))
        shape = (n_g, S5_STATE, cs * S5_GROUP)
        return re.reshape(shape), -im.reshape(shape)

    if_re, if_im = state_in(pos + 1, 0)
    ib_re, ib_im = state_in(cs - pos, 1)
    w_si = jnp.concatenate([if_re, ib_re, if_im, ib_im], axis=1)
    lam_c = jnp.concatenate([p_re[cs, 0], p_re[cs, 1], p_im[cs, 0], p_im[cs, 1]], axis=1)
    return w_so.astype(BF16), w_in.astype(BF16), w_si.astype(BF16), lam_c


def _outproj_kernel(x_ref, ga_ref, sa_ref, y5_ref, u_ref, d5_ref, wglu_ref, bglu_ref,
                    w0_ref, w1_ref, w2_ref, nw_ref, gate_ref, o_ref, *, nb):
    y5 = y5_ref[...] + u_ref[...] * d5_ref[...]
    ge = y5 * (0.5 * (1.0 + jnp.tanh(math.sqrt(2.0 / math.pi) * (y5 + 0.044715 * (y5 * y5 * y5)))))
    s5 = ge * _sigmoid(_bdot(ge, wglu_ref[...]) + bglu_ref[...])
    acc = _bdot(ga_ref[...], w0_ref[...]) + _bdot(sa_ref[...], w1_ref[...]) + _bdot(s5, w2_ref[...])
    r = _rms(acc, nw_ref[...])
    for s in range(nb):
        rows = slice(s * ROW_BLOCK, (s + 1) * ROW_BLOCK)
        o_ref[rows, :] = x_ref[rows, :] + gate_ref[s] * r[rows]


def _outproj_call(x, gdn_o, ssd_o, y5, h, d5, wglu, bglu, w_out, nw, modtab, tm):
    t, d = x.shape
    nb = tm // ROW_BLOCK
    u0 = COL_U // S5_DIM
    row = lambda i: (i, 0)
    const = lambda i: (0, 0)
    return pl.pallas_call(
        functools.partial(_outproj_kernel, nb=nb),
        out_shape=jax.ShapeDtypeStruct((t, d), F32),
        grid=(t // tm,),
        in_specs=[
            pl.BlockSpec((tm, d), row),
            pl.BlockSpec((tm, GDN_DIM), row),
            pl.BlockSpec((tm, SSM_DIM), row),
            pl.BlockSpec((tm, S5_DIM), row),
            pl.BlockSpec((tm, S5_DIM), lambda i: (i, u0)),
            pl.BlockSpec((1, S5_DIM), const),
            pl.BlockSpec((S5_DIM, S5_DIM), const),
            pl.BlockSpec((1, S5_DIM), const),
            pl.BlockSpec((GDN_DIM, d), const),
            pl.BlockSpec((SSM_DIM, d), lambda i: (1, 0)),
            pl.BlockSpec((S5_DIM, d), lambda i: ((GDN_DIM + SSM_DIM) // S5_DIM, 0)),
            pl.BlockSpec((1, d), const),
            pl.BlockSpec((nb, 1, d), lambda i: (i, 0, 2)),
        ],
        out_specs=pl.BlockSpec((tm, d), row),
        input_output_aliases={0: 0},
        compiler_params=_params("parallel"),
        name="out_proj",
    )(x, gdn_o, ssd_o, y5, h, d5, wglu, bglu, w_out, w_out, w_out, nw, modtab)


def _ffn_kernel(x_ref, nw_ref, sh_ref, sc_ref, wg_ref, wu_ref, wo_ref, nwo_ref, gate_ref, o_ref,
                xn_ref, acc_ref, *, nb):
    j = pl.program_id(1)

    @pl.when(j == 0)
    def _():
        y = _rms(x_ref[...], nw_ref[...])
        for s in range(nb):
            r = slice(s * ROW_BLOCK, (s + 1) * ROW_BLOCK)
            xn_ref[r, :] = (y[r] * (1.0 + sc_ref[s]) + sh_ref[s]).astype(BF16)
        acc_ref[...] = jnp.zeros(acc_ref.shape, F32)

    xn = xn_ref[...]
    gate = jnp.dot(xn, wg_ref[...], preferred_element_type=F32)
    up = jnp.dot(xn, wu_ref[...], preferred_element_type=F32)
    acc_ref[...] += _bdot(_silu(gate) * up, wo_ref[...])

    @pl.when(j == pl.num_programs(1) - 1)
    def _():
        r = _rms(acc_ref[...], nwo_ref[...])
        for s in range(nb):
            rows = slice(s * ROW_BLOCK, (s + 1) * ROW_BLOCK)
            o_ref[rows, :] = x_ref[rows, :] + gate_ref[s] * r[rows]


def _ffn_call(x, nw, nwo, modtab, w_in, w_out, tm, tf):
    t, d = x.shape
    f = w_out.shape[0]
    nb = tm // ROW_BLOCK
    nf = f // tf
    return pl.pallas_call(
        functools.partial(_ffn_kernel, nb=nb),
        out_shape=jax.ShapeDtypeStruct((t, d), F32),
        grid=(t // tm, nf),
        in_specs=[
            pl.BlockSpec((tm, d), lambda i, j: (i, 0)),
            pl.BlockSpec((1, d), lambda i, j: (0, 0)),
            pl.BlockSpec((nb, 1, d), lambda i, j: (i, 0, 3)),
            pl.BlockSpec((nb, 1, d), lambda i, j: (i, 0, 4)),
            pl.BlockSpec((d, tf), lambda i, j: (0, j)),
            pl.BlockSpec((d, tf), lambda i, j: (0, nf + j)),
            pl.BlockSpec((tf, d), lambda i, j: (j, 0)),
            pl.BlockSpec((1, d), lambda i, j: (0, 0)),
            pl.BlockSpec((nb, 1, d), lambda i, j: (i, 0, 5)),
        ],
        out_specs=pl.BlockSpec((tm, d), lambda i, j: (i, 0)),
        scratch_shapes=[pltpu.VMEM((tm, d), BF16), pltpu.VMEM((tm, d), F32)],
        input_output_aliases={0: 0},
        compiler_params=_params("parallel", "arbitrary"),
        name="swiglu_ffn",
    )(x, nw, modtab, modtab, w_in, w_in, w_out, nwo, modtab)


def _largest_tile(total, cap):
    tile = ROW_BLOCK
    while tile * 2 <= cap and total % (tile * 2) == 0:
        tile *= 2
    return tile


def kernel(x, c, ctx, c_ctx, w_mod, b_mod, norm_w, w_in, w_out, gdn_conv, gdn_a_log, gdn_dt_bias, gdn_norm, ssm_conv, ssm_a_log, ssm_dt_bias, ssm_d, ssm_norm, s5_lam_re, s5_lam_im, s5_log_step, s5_b_re, s5_b_im, s5_c_re, s5_c_im, s5_d, s5_w_glu, s5_b_glu, w_ffn_in, w_ffn_out):
    batch, seq, d = x.shape
    ctx_len = ctx.shape[1]
    depth = w_mod.shape[0]
    assert ctx_len == ROW_BLOCK and seq % ROW_BLOCK == 0 and d % LANES == 0
    lc = ctx_len + seq
    t = batch * lc
    blocks_per_seq = lc // ROW_BLOCK
    ctx_blocks = ctx_len // ROW_BLOCK
    n_blocks = t // ROW_BLOCK
    n_mix_chunks = lc // MIX_CHUNK
    n_s5_chunks = lc // S5_CHUNK

    xs = jnp.concatenate([ctx, x], axis=1).reshape(t, d)

    mod_rows = ((batch + 1 + 7) // 8) * 8
    cvec = jnp.zeros((mod_rows, d), F32).at[:batch].set(c).at[batch].set(c_ctx)
    mod = _mod_call(cvec, w_mod, b_mod)
    blk = jnp.arange(n_blocks)
    mod_row = jnp.where(blk % blocks_per_seq < ctx_blocks, batch, blk // blocks_per_seq)

    tm_in = _largest_tile(t, 512)
    tm_out = ROW_BLOCK
    tm_ffn = _largest_tile(t, 512)

    o_qkv = 0
    o_gz = o_qkv + 3 * GDN_DIM
    o_ga = o_gz + GDN_DIM
    o_gb = o_ga + 2 * GDN_HEADS
    o_sz = o_gb + 2 * GDN_HEADS
    o_xbc = o_sz + SSM_DIM
    o_sdt = o_xbc + SSM_CONV_DIM
    o_u = o_sdt + 2 * SSM_HEADS

    for i in range(depth):
        modtab = mod[i][mod_row].reshape(n_blocks, 1, 6 * d)
        wi = w_in[i]
        w_main = jnp.concatenate([wi[:, o_qkv:o_ga], wi[:, o_sz:o_sdt], wi[:, o_u:o_u + S5_DIM]],
                                 axis=1).astype(BF16)
        w_gate = jnp.concatenate([wi[:, o_ga:o_sz], wi[:, o_sdt:o_u]], axis=1)
        w_gate = jnp.pad(w_gate, ((0, 0), (0, GATE_DIM - w_gate.shape[1]))).astype(BF16)
        h, hg, hgt = _inproj_call(xs, norm_w[i, 0][None], modtab, w_main, w_gate, w_gate.T,
                                  tm_in, 512)

        qkv = _prep_call(h, COL_QKV, GDN_DIM, 3, gdn_conv[i], 1, 1, GDN_HEAD_DIM ** -0.5,
                         blocks_per_seq, ctx_blocks)
        hh = jnp.arange(GDN_HEADS)
        gsel = jnp.stack([GATE_A + hh, GATE_A + GDN_HEADS + hh, GATE_B + hh, GATE_B + GDN_HEADS + hh], axis=1)
        gcol = jnp.pad(jnp.transpose(hg[:, gsel], (1, 0, 2)), ((0, 0), (0, 0), (0, 4)))
        grow = jnp.pad(hgt[gsel], ((0, 0), (0, 4), (0, 0)))
        grow = jnp.transpose(grow.reshape(GDN_HEADS, 8, t // MIX_CHUNK, MIX_CHUNK), (0, 2, 1, 3))
        hp4 = jnp.concatenate([gdn_a_log[i], gdn_dt_bias[i]], axis=0).T
        hp = jnp.broadcast_to(jnp.pad(hp4, ((0, 0), (0, 4)))[:, :, None], (GDN_HEADS, 8, LANES))
        hpc = jnp.zeros((GDN_HEADS, 8, 8), F32)
        hpc = hpc.at[:, 0, 0:2].set(gdn_a_log[i].T).at[:, 1, 0:2].set(gdn_dt_bias[i].T)
        gdn_o = _gdn_call(qkv, h, gcol, grow, hp, hpc, gdn_norm[i][None], batch, lc,
                          ctx_len // MIX_CHUNK)

        xbc = _prep_call(h, COL_XBC, SSM_CONV_DIM, 1, ssm_conv[i], 0, 0, 1.0, blocks_per_seq, ctx_blocks)
        gh = jnp.arange(SSM_GROUP_HEADS)
        dsel = jnp.stack([jnp.concatenate([GATE_DT + g * SSM_GROUP_HEADS + gh,
                                           GATE_DT + SSM_HEADS + g * SSM_GROUP_HEADS + gh])
                          for g in range(SSM_GROUPS)])
        dcol = jnp.pad(jnp.transpose(hg[:, dsel], (1, 0, 2)), ((0, 0), (0, 0), (0, 4)))
        drow = jnp.pad(hgt[dsel], ((0, 0), (0, 4), (0, 0)))
        drow = jnp.transpose(drow.reshape(SSM_GROUPS, 16, t // MIX_CHUNK, MIX_CHUNK), (0, 2, 1, 3))
        sp = jnp.stack([ssm_a_log[i].reshape(2, SSM_GROUPS, SSM_GROUP_HEADS),
                        ssm_dt_bias[i].reshape(2, SSM_GROUPS, SSM_GROUP_HEADS)])
        sp = jnp.transpose(sp, (2, 0, 1, 3)).reshape(SSM_GROUPS, 2, 2 * SSM_GROUP_HEADS)
        prow = jnp.pad(sp, ((0, 0), (0, 6), (0, 4)))
        pcol = jnp.pad(jnp.transpose(sp, (0, 2, 1)), ((0, 0), (0, 4), (0, 6)))
        dskip = jnp.repeat(ssm_d[i], SSM_HEAD_DIM).reshape(SSM_GROUPS, 1, SSM_GROUP_DIM)
        ssd_o = _ssd_call(xbc, h, dcol, drow, prow, pcol, dskip,
                          ssm_norm[i].reshape(SSM_GROUPS, 1, SSM_GROUP_DIM), batch, lc,
                          ctx_len // MIX_CHUNK)

        w_so, w_intra, w_si, lam_c = _s5_operators(s5_lam_re[i], s5_lam_im[i], s5_log_step[i],
                                                   s5_b_re[i], s5_b_im[i], s5_c_re[i], s5_c_im[i])
        u = h[:, COL_U:COL_U + S5_DIM].astype(BF16)
        u_t = jnp.transpose(u.reshape(batch, n_s5_chunks, S5_CHUNK, S5_GROUPS, S5_GROUP), (0, 3, 1, 2, 4))
        u_t = u_t.reshape(batch, S5_GROUPS, n_s5_chunks, S5_CHUNK * S5_GROUP)
        y_t = _s5_call(u_t, w_so, w_intra, w_si, lam_c, ctx_len // S5_CHUNK, 16)
        y5 = jnp.transpose(y_t.reshape(batch, S5_GROUPS, n_s5_chunks, S5_CHUNK, S5_GROUP), (0, 2, 3, 1, 4))
        y5 = y5.reshape(t, S5_DIM)

        xs = _outproj_call(xs, gdn_o, ssd_o, y5, h, s5_d[i][None], s5_w_glu[i].astype(BF16),
                           s5_b_glu[i][None], w_out[i].astype(BF16), norm_w[i, 1][None], modtab, tm_out)
        xs = _ffn_call(xs, norm_w[i, 2][None], norm_w[i, 3][None], modtab,
                       w_ffn_in[i].astype(BF16), w_ffn_out[i].astype(BF16), tm_ffn, 512)

    return xs.reshape(batch, lc, d)[:, ctx_len:]
```

```python
import functools
import math

import jax
import jax.numpy as jnp
from jax import lax
from jax.experimental import pallas as pl
from jax.experimental.pallas import tpu as pltpu

F32 = jnp.float32
BF16 = jnp.bfloat16
HIGHEST = lax.Precision.HIGHEST

RMS_EPS = 1e-6
GRID_W = 64
SHORT_CONV = 5

GDN_HEADS = 6
GDN_HEAD_DIM = 128
GDN_DIM = GDN_HEADS * GDN_HEAD_DIM
SSM_HEADS = 12
SSM_HEAD_DIM = 64
SSM_DIM = SSM_HEADS * SSM_HEAD_DIM
SSM_GROUPS = 2
SSM_STATE = 128
SSM_CONV_DIM = SSM_DIM + 2 * SSM_GROUPS * SSM_STATE
SSM_GROUP_HEADS = SSM_HEADS // SSM_GROUPS
SSM_GROUP_DIM = SSM_DIM // SSM_GROUPS
S5_GROUPS = 32
S5_GROUP = 16
S5_DIM = S5_GROUPS * S5_GROUP
S5_STATE = 64
MIX_DIM = GDN_DIM + SSM_DIM + S5_DIM

COL_QKV = 0
COL_GZ = 3 * GDN_DIM
COL_SZ = COL_GZ + GDN_DIM
COL_XBC = COL_SZ + SSM_DIM
COL_U = COL_XBC + SSM_CONV_DIM
MAIN_DIM = COL_U + S5_DIM
GATE_DIM = 128
GATE_A = 0
GATE_B = 2 * GDN_HEADS
GATE_DT = 4 * GDN_HEADS

LANES = 128
ROW_BLOCK = 256
MIX_CHUNK = 128
GDN_PREP_UNROLL = 2
S5_CHUNK = 16
VMEM_LIMIT = 56 * 1024 * 1024
NEG_BIG = -1e30


def _sigmoid(v):
    return 1.0 / (1.0 + jnp.exp(-v))


def _silu(v):
    return v * _sigmoid(v)


def _softplus(v):
    return jnp.maximum(v, 0.0) + jnp.log(1.0 + jnp.exp(-jnp.abs(v)))


def _rms(v, w):
    return v * lax.rsqrt(jnp.mean(v * v, axis=-1, keepdims=True) + RMS_EPS) * w


def _bdot(a, b):
    return jnp.dot(a.astype(BF16), b.astype(BF16), preferred_element_type=F32)


def _fdot(a, b):
    return jnp.dot(a, b, precision=HIGHEST, preferred_element_type=F32)


def _dot16(a, b):
    return jnp.dot(a, b, preferred_element_type=F32)


SOLVE_BASE = 8


def _unit_triangular_solves(ms, ii, jj, rhss):
    n = ms[0].shape[0]

    def same_block(size):
        shift = int(math.log2(size))
        return jnp.right_shift(ii, shift) == jnp.right_shift(jj, shift)

    eye = (ii == jj).astype(F32)
    powers = [jnp.where(same_block(SOLVE_BASE), m, 0.0) for m in ms]
    invs = [eye - d for d in powers]
    for _ in range(int(math.log2(SOLVE_BASE)) - 1):
        powers = [_fdot(a, a) for a in powers]
        invs = [inv + _fdot(inv, a) for inv, a in zip(invs, powers)]
    size = SOLVE_BASE
    while size < n:
        off_diag = jnp.logical_and(same_block(2 * size), jnp.logical_not(same_block(size)))
        coupled = [_fdot(jnp.where(off_diag, m, 0.0), inv) for m, inv in zip(ms, invs)]
        invs = [inv - _fdot(inv, c) for inv, c in zip(invs, coupled)]
        size *= 2
    return [_fdot(inv, rhs) for inv, rhs in zip(invs, rhss)]


def _params(*sem):
    return pltpu.CompilerParams(dimension_semantics=sem, vmem_limit_bytes=VMEM_LIMIT)


def _mod_kernel(c_ref, w_ref, b_ref, o_ref):
    o_ref[0] = _fdot(_silu(c_ref[...]), w_ref[0]) + b_ref[0]


def _mod_call(cvec, w_mod, b_mod):
    depth, d_model, n = w_mod.shape
    rows = cvec.shape[0]
    tn = 1024
    return pl.pallas_call(
        _mod_kernel,
        out_shape=jax.ShapeDtypeStruct((depth, rows, n), F32),
        grid=(depth, n // tn),
        in_specs=[
            pl.BlockSpec((rows, d_model), lambda l, j: (0, 0)),
            pl.BlockSpec((1, d_model, tn), lambda l, j: (l, 0, j)),
            pl.BlockSpec((1, 1, tn), lambda l, j: (l, 0, j)),
        ],
        out_specs=pl.BlockSpec((1, rows, tn), lambda l, j: (l, 0, j)),
        compiler_params=_params("parallel", "parallel"),
        name="adaln_mod",
    )(cvec, w_mod, b_mod.reshape(depth, 1, n))


def _inproj_kernel(x_ref, nw_ref, sh_ref, sc_ref, w_ref, wg_ref, wgt_ref,
                   h_ref, hg_ref, hgt_ref, xn_ref, *, nb):
    @pl.when(pl.program_id(1) == 0)
    def _():
        y = _rms(x_ref[...], nw_ref[...])
        for s in range(nb):
            r = slice(s * ROW_BLOCK, (s + 1) * ROW_BLOCK)
            xn_ref[r, :] = (y[r] * (1.0 + sc_ref[s]) + sh_ref[s]).astype(BF16)
        xn = xn_ref[...]
        hg_ref[...] = jnp.dot(xn, wg_ref[...], preferred_element_type=F32)
        hgt_ref[...] = lax.dot_general(wgt_ref[...], xn, (((1,), (1,)), ((), ())),
                                       preferred_element_type=F32)

    h_ref[...] = jnp.dot(xn_ref[...], w_ref[...], preferred_element_type=F32).astype(h_ref.dtype)


def _inproj_call(x, nw, modtab, w_main, w_gate, w_gate_t, tm, tn):
    t, d = x.shape
    n = w_main.shape[1]
    nb = tm // ROW_BLOCK
    return pl.pallas_call(
        functools.partial(_inproj_kernel, nb=nb),
        out_shape=(jax.ShapeDtypeStruct((t, n), BF16),
                   jax.ShapeDtypeStruct((t, GATE_DIM), F32),
                   jax.ShapeDtypeStruct((GATE_DIM, t), F32)),
        grid=(t // tm, n // tn),
        in_specs=[
            pl.BlockSpec((tm, d), lambda i, j: (i, 0)),
            pl.BlockSpec((1, d), lambda i, j: (0, 0)),
            pl.BlockSpec((nb, 1, d), lambda i, j: (i, 0, 0)),
            pl.BlockSpec((nb, 1, d), lambda i, j: (i, 0, 1)),
            pl.BlockSpec((d, tn), lambda i, j: (0, j)),
            pl.BlockSpec((d, GATE_DIM), lambda i, j: (0, 0)),
            pl.BlockSpec((GATE_DIM, d), lambda i, j: (0, 0)),
        ],
        out_specs=(pl.BlockSpec((tm, tn), lambda i, j: (i, j)),
                   pl.BlockSpec((tm, GATE_DIM), lambda i, j: (i, 0)),
                   pl.BlockSpec((GATE_DIM, tm), lambda i, j: (0, i))),
        scratch_shapes=[pltpu.VMEM((tm, d), BF16)],
        compiler_params=_params("parallel", "arbitrary"),
        name="in_proj",
    )(x, nw, modtab, modtab, w_main, w_gate, w_gate_t)


def _prep_kernel(h_ref, w_ref, o_ref, *, n_q, n_k, q_scale, blocks_per_seq, ctx_blocks):
    i = pl.program_id(0)
    j = pl.program_id(1)
    x = h_ref[...].astype(F32)
    width = x.shape[1]
    t = lax.broadcasted_iota(jnp.int32, (ROW_BLOCK, 1), 0)
    is_ctx = (i % blocks_per_seq) < ctx_blocks
    seg = jnp.where(is_ctx, ROW_BLOCK, GRID_W)
    pos = jnp.bitwise_and(t, seg - 1)
    half = SHORT_CONV // 2
    acc = x * w_ref[half:half + 1, :]
    for dlt in range(-half, half + 1):
        if dlt == 0:
            continue
        shifted = pltpu.roll(x, (-dlt) % ROW_BLOCK, axis=0)
        ok = jnp.logical_and(pos + dlt >= 0, pos + dlt < seg)
        acc = acc + jnp.where(ok, shifted, 0.0) * w_ref[dlt + half:dlt + half + 1, :]
    y = _silu(acc)
    norm_scale = jnp.where(j < n_q, q_scale, 1.0)
    use_norm = j < n_q + n_k
    for b in range(width // LANES):
        c = slice(b * LANES, (b + 1) * LANES)
        yb = y[:, c]
        inv = lax.rsqrt(jnp.sum(yb * yb, axis=-1, keepdims=True) + 1e-6) * norm_scale
        o_ref[:, c] = yb * jnp.where(use_norm, inv, 1.0)


def _prep_call(h, col0, width, n_col_blocks, conv_w, n_q, n_k, q_scale, blocks_per_seq, ctx_blocks):
    t = h.shape[0]
    cw = jnp.zeros((8, conv_w.shape[1]), F32).at[:SHORT_CONV].set(conv_w)
    cb0 = col0 // width
    return pl.pallas_call(
        functools.partial(_prep_kernel, n_q=n_q, n_k=n_k, q_scale=q_scale,
                          blocks_per_seq=blocks_per_seq, ctx_blocks=ctx_blocks),
        out_shape=jax.ShapeDtypeStruct((t, width * n_col_blocks), F32),
        grid=(t // ROW_BLOCK, n_col_blocks),
        in_specs=[
            pl.BlockSpec((ROW_BLOCK, width), lambda i, j: (i, cb0 + j)),
            pl.BlockSpec((8, width), lambda i, j: (0, j)),
        ],
        out_specs=pl.BlockSpec((ROW_BLOCK, width), lambda i, j: (i, j)),
        compiler_params=_params("parallel", "parallel"),
        name="conv_prep",
    )(h, cw)


def _bwd_chunk(s, n_chunks, n_ctx_chunks):
    return jnp.where(s < n_ctx_chunks, n_ctx_chunks - 1 - s, n_chunks + n_ctx_chunks - 1 - s)


def _tri_masks(c):
    ii = lax.broadcasted_iota(jnp.int32, (c, c), 0)
    jj = lax.broadcasted_iota(jnp.int32, (c, c), 1)
    return ii, jj


def _gdn_kernel(q_ref, k_ref, v_ref, z_ref, gc_ref, gr_ref, hp_ref, hpc_ref, nw_ref, o_ref,
                u_ref, wq_ref, qk_ref, kdt_ref, gt_ref, s_ref, *, n_chunks, n_ctx_chunks):
    c_len = MIX_CHUNK
    hd = GDN_HEAD_DIM
    ii, jj = _tri_masks(c_len)
    lower = (ii >= jj).astype(F32)
    upper = (ii <= jj).astype(F32)

    neg_a_row = -jnp.exp(hp_ref[0, 0:2, :])
    bias_row = hp_ref[0, 2:4, :]
    neg_a_col = -jnp.exp(hpc_ref[0, 0:1, 0:2])
    bias_col = hpc_ref[0, 1:2, 0:2]

    def prepare(i, carry):
        chunks = [i * GDN_PREP_UNROLL + k for k in range(GDN_PREP_UNROLL)]
        rows = [pl.ds(pl.multiple_of(c * c_len, c_len), c_len) for c in chunks]
        qs = [q_ref[r, :] for r in rows]
        ks = [k_ref[r, :] for r in rows]
        vs = [v_ref[r, :] for r in rows]
        kts = [kc.T for kc in ks]
        ktbs = [ktc.astype(BF16) for ktc in kts]
        kks = [_dot16(kc.astype(BF16), ktb) for kc, ktb in zip(ks, ktbs)]
        qkts = [_dot16(qc.astype(BF16), ktb) for qc, ktb in zip(qs, ktbs)]
        g_cols = [neg_a_col * _softplus(gc_ref[0, r, 0:2] + bias_col) for r in rows]
        betas = [_sigmoid(gc_ref[0, r, 2:4]) for r in rows]
        g_rows = [neg_a_row * _softplus(gr_ref[0, c][0:2, :] + bias_row) for c in chunks]
        probs = [(k, d) for k in range(GDN_PREP_UNROLL) for d in range(2)]
        cum_cols = [_fdot(lower if d == 0 else upper,
                          jnp.broadcast_to(g_cols[k][:, d:d + 1], (c_len, LANES))) for k, d in probs]
        cum_rows = [_fdot(jnp.broadcast_to(g_rows[k][d:d + 1], (8, c_len)),
                          upper if d == 0 else lower)[0:1, :] for k, d in probs]
        gammas = [jnp.exp(jnp.where((ii >= jj) if d == 0 else (ii <= jj), cc - cr, NEG_BIG))
                  for (k, d), cc, cr in zip(probs, cum_cols, cum_rows)]
        ms = [jnp.where((ii > jj) if d == 0 else (ii < jj), kks[k] * gm, 0.0) * betas[k][:, d:d + 1]
              for (k, d), gm in zip(probs, gammas)]
        ecums = [jnp.exp(cc) for cc in cum_cols]
        rhss = [jnp.concatenate([vs[k] * betas[k][:, d:d + 1], ks[k] * (betas[k][:, d:d + 1] * ec)], axis=1)
                for (k, d), ec in zip(probs, ecums)]
        sols = _unit_triangular_solves(ms, ii, jj, rhss)
        for (k, d), sol, gm, ec, cr in zip(probs, sols, gammas, ecums, cum_rows):
            c = chunks[k]
            cend = cr[:, c_len - 1:c_len] if d == 0 else cr[:, 0:1]
            u_ref[d, rows[k], :] = sol[:, :hd]
            wq_ref[d, c, 0:c_len, :] = sol[:, hd:].astype(BF16)
            wq_ref[d, c, c_len:2 * c_len, :] = (qs[k] * ec).astype(BF16)
            qk_ref[d, c] = (qkts[k] * gm).astype(BF16)
            kdt_ref[d, c] = (kts[k] * jnp.exp(cend - cr)).astype(BF16)
            gt_ref[d, c] = jnp.broadcast_to(jnp.exp(cend), (8, LANES))
        return carry

    lax.fori_loop(0, n_chunks // GDN_PREP_UNROLL, prepare, 0)
    s_ref[...] = jnp.zeros(s_ref.shape, F32)
    o_ref[...] = jnp.zeros(o_ref.shape, F32)

    def advance(d, c):
        rows = pl.ds(pl.multiple_of(c * c_len, c_len), c_len)
        state = s_ref[d]
        ws_qs = _dot16(wq_ref[d, c], state.astype(BF16))
        v_new = (u_ref[d, rows, :] - ws_qs[:c_len]).astype(BF16)
        o_ref[rows, :] += ws_qs[c_len:] + _dot16(qk_ref[d, c], v_new)
        s_ref[d] = state * gt_ref[d, c][0:1, :] + _dot16(kdt_ref[d, c], v_new)

    def step(s, carry):
        advance(0, s)
        advance(1, _bwd_chunk(s, n_chunks, n_ctx_chunks))
        return carry

    lax.fori_loop(0, n_chunks, step, 0)

    def finish(c, carry):
        rows = pl.ds(pl.multiple_of(c * c_len, c_len), c_len)
        o_ref[rows, :] = _rms(o_ref[rows, :], nw_ref[...]) * _silu(z_ref[rows, :].astype(F32))
        return carry

    lax.fori_loop(0, n_chunks, finish, 0)


def _gdn_call(qkv, h, gcol, grow, hp, hpc, nw, batch, lc, n_ctx_chunks):
    t = qkv.shape[0]
    n_chunks = lc // MIX_CHUNK
    hd = GDN_HEAD_DIM
    z0 = COL_GZ // hd
    return pl.pallas_call(
        functools.partial(_gdn_kernel, n_chunks=n_chunks, n_ctx_chunks=n_ctx_chunks),
        out_shape=jax.ShapeDtypeStruct((t, GDN_DIM), F32),
        grid=(batch, GDN_HEADS),
        in_specs=[
            pl.BlockSpec((lc, hd), lambda b, hh: (b, hh)),
            pl.BlockSpec((lc, hd), lambda b, hh: (b, GDN_HEADS + hh)),
            pl.BlockSpec((lc, hd), lambda b, hh: (b, 2 * GDN_HEADS + hh)),
            pl.BlockSpec((lc, hd), lambda b, hh: (b, z0 + hh)),
            pl.BlockSpec((1, lc, 8), lambda b, hh: (hh, b, 0)),
            pl.BlockSpec((1, n_chunks, 8, MIX_CHUNK), lambda b, hh: (hh, b, 0, 0)),
            pl.BlockSpec((1, 8, LANES), lambda b, hh: (hh, 0, 0)),
            pl.BlockSpec((1, 8, 8), lambda b, hh: (hh, 0, 0)),
            pl.BlockSpec((1, hd), lambda b, hh: (0, 0)),
        ],
        out_specs=pl.BlockSpec((lc, hd), lambda b, hh: (b, hh)),
        scratch_shapes=[pltpu.VMEM((2, lc, hd), F32),
                        pltpu.VMEM((2, n_chunks, 2 * MIX_CHUNK, hd), BF16),
                        pltpu.VMEM((2, n_chunks, MIX_CHUNK, MIX_CHUNK), BF16),
                        pltpu.VMEM((2, n_chunks, hd, MIX_CHUNK), BF16),
                        pltpu.VMEM((2, n_chunks, 8, LANES), F32),
                        pltpu.VMEM((2, hd, hd), F32)],
        compiler_params=_params("parallel", "parallel"),
        name="gdn_mixer",
    )(qkv, qkv, qkv, h, gcol, grow, hp, hpc, nw)


def _ssd_kernel(x_ref, b_ref, c_ref, z_ref, dc_ref, dr_ref, pr_ref, pc_ref, dsk_ref, nw_ref, o_ref,
                bt_ref, cb_ref, st_ref, *, n_chunks, n_ctx_chunks):
    c_len = MIX_CHUNK
    hg = SSM_GROUP_HEADS
    hp = SSM_HEAD_DIM
    ii, jj = _tri_masks(c_len)
    lower = (ii >= jj).astype(F32)
    upper = (ii <= jj).astype(F32)

    def prepare(c, carry):
        rows = pl.ds(pl.multiple_of(c * c_len, c_len), c_len)
        btc = b_ref[rows, :].T
        bt_ref[c] = btc
        cb_ref[c] = _bdot(c_ref[rows, :], btc)
        return carry

    lax.fori_loop(0, n_chunks, prepare, 0)
    st_ref[...] = jnp.zeros(st_ref.shape, F32)
    o_ref[...] = jnp.zeros(o_ref.shape, F32)

    neg_a_row = -jnp.exp(pr_ref[0, 0:1, :])
    bias_row = pr_ref[0, 1:2, :]
    neg_a_col = -jnp.exp(pc_ref[0, :, 0:1])
    bias_col = pc_ref[0, :, 1:2]

    def step(s, carry):
        dirs = range(2)
        chunks = [s, _bwd_chunk(s, n_chunks, n_ctx_chunks)]
        rows = [pl.ds(pl.multiple_of(c * c_len, c_len), c_len) for c in chunks]
        xcs = [x_ref[r, :] for r in rows]
        dtcs = [_softplus(dc_ref[0, r, :] + bias_row) for r in rows]
        dtrs = [_softplus(dr_ref[0, c] + bias_col) for c in chunks]
        cum_cols = [_fdot(lower if d == 0 else upper, dtcs[d] * neg_a_row) for d in dirs]
        cum_rows = [_fdot(dtrs[d] * neg_a_col, upper if d == 0 else lower) for d in dirs]
        states = [st_ref[d] for d in dirs]
        y_offs = [_bdot(c_ref[rows[d], :], states[d]) for d in dirs]
        cbs = [cb_ref[c] for c in chunks]
        cends = [cum_cols[0][c_len - 1:c_len, :], cum_cols[1][0:1, :]]
        to_ends = [jnp.exp(cends[d] - cum_cols[d]) for d in dirs]
        ecums = [jnp.exp(cum_cols[d]) for d in dirs]
        gtots = [jnp.exp(cends[d]) for d in dirs]
        heads = [(d, hh) for d in dirs for hh in range(hg)]
        decays = [jnp.exp(jnp.where((ii >= jj) if d == 0 else (ii <= jj),
                                    cum_cols[d][:, d * hg + hh:d * hg + hh + 1]
                                    - cum_rows[d][d * hg + hh:d * hg + hh + 1, :], NEG_BIG))
                  for d, hh in heads]
        xdts = [xcs[d][:, hh * hp:(hh + 1) * hp] * dtcs[d][:, d * hg + hh:d * hg + hh + 1] for d, hh in heads]
        y_diags = [_bdot(cbs[d] * dec, xdt) for (d, hh), dec, xdt in zip(heads, decays, xdts)]
        for d in dirs:
            ys, xds, gts = [], [], []
            for hh in range(hg):
                col = d * hg + hh
                k = d * hg + hh
                ys.append(y_diags[k] + ecums[d][:, col:col + 1] * y_offs[d][:, hh * hp:(hh + 1) * hp])
                xds.append(xdts[k] * to_ends[d][:, col:col + 1])
                gts.append(jnp.broadcast_to(gtots[d][:, col:col + 1], (1, hp)))
            st_ref[d] = (states[d] * jnp.concatenate(gts, axis=1)
                         + _bdot(bt_ref[chunks[d]], jnp.concatenate(xds, axis=1)))
            o_ref[rows[d], :] += jnp.concatenate(ys, axis=1)
        return carry

    lax.fori_loop(0, n_chunks, step, 0)

    def finish(c, carry):
        rows = pl.ds(pl.multiple_of(c * c_len, c_len), c_len)
        y = o_ref[rows, :] + x_ref[rows, :] * dsk_ref[0]
        o_ref[rows, :] = _rms(y * _silu(z_ref[rows, :].astype(F32)), nw_ref[0])
        return carry

    lax.fori_loop(0, n_chunks, finish, 0)


def _ssd_call(xbc, h, dcol, drow, prow, pcol, dskip, nw, batch, lc, n_ctx_chunks):
    t = xbc.shape[0]
    n_chunks = lc // MIX_CHUNK
    gd = SSM_GROUP_DIM
    ns = SSM_STATE
    b0 = SSM_DIM // ns
    c0 = b0 + SSM_GROUPS
    z0 = COL_SZ // gd
    return pl.pallas_call(
        functools.partial(_ssd_kernel, n_chunks=n_chunks, n_ctx_chunks=n_ctx_chunks),
        out_shape=jax.ShapeDtypeStruct((t, SSM_DIM), F32),
        grid=(batch, SSM_GROUPS),
        in_specs=[
            pl.BlockSpec((lc, gd), lambda b, g: (b, g), pipeline_mode=pl.Buffered(1)),
            pl.BlockSpec((lc, ns), lambda b, g: (b, b0 + g), pipeline_mode=pl.Buffered(1)),
            pl.BlockSpec((lc, ns), lambda b, g: (b, c0 + g), pipeline_mode=pl.Buffered(1)),
            pl.BlockSpec((lc, gd), lambda b, g: (b, z0 + g), pipeline_mode=pl.Buffered(1)),
            pl.BlockSpec((1, lc, 16), lambda b, g: (g, b, 0), pipeline_mode=pl.Buffered(1)),
            pl.BlockSpec((1, n_chunks, 16, MIX_CHUNK), lambda b, g: (g, b, 0, 0)),
            pl.BlockSpec((1, 8, 16), lambda b, g: (g, 0, 0)),
            pl.BlockSpec((1, 16, 8), lambda b, g: (g, 0, 0)),
            pl.BlockSpec((1, 1, gd), lambda b, g: (g, 0, 0)),
            pl.BlockSpec((1, 1, gd), lambda b, g: (g, 0, 0)),
        ],
        out_specs=pl.BlockSpec((lc, gd), lambda b, g: (b, g)),
        scratch_shapes=[pltpu.VMEM((n_chunks, ns, MIX_CHUNK), F32),
                        pltpu.VMEM((n_chunks, MIX_CHUNK, MIX_CHUNK), F32),
                        pltpu.VMEM((2, ns, gd), F32)],
        compiler_params=_params("parallel", "parallel"),
        name="ssd_mixer",
    )(xbc, xbc, xbc, h, dcol, drow, prow, pcol, dskip, nw)


def _s5_kernel(u_ref, wso_ref, win_ref, wsi_ref, lam_ref, y_ref, ere_ref, eim_ref, xre_ref, xim_ref, *,
               n_groups, n_chunks, n_ctx_chunks):
    half = 2 * S5_STATE
    for g in range(n_groups):
        e = jnp.dot(u_ref[0, g], wso_ref[g], preferred_element_type=F32)
        ere_ref[pl.ds(g, n_chunks, stride=n_groups), :] = e[:, :half]
        eim_ref[pl.ds(g, n_chunks, stride=n_groups), :] = e[:, half:]

    lam_re = lam_ref[:, :half]
    lam_im = lam_ref[:, half:]
    is_fwd = lax.broadcasted_iota(jnp.int32, (n_groups, half), 1) < S5_STATE

    def step(s, carry):
        x_re, x_im = carry
        cf = pl.multiple_of(s * n_groups, n_groups)
        cb = pl.multiple_of(_bwd_chunk(s, n_chunks, n_ctx_chunks) * n_groups, n_groups)
        rows_f = pl.ds(cf, n_groups)
        rows_b = pl.ds(cb, n_groups)
        xre_ref[rows_f, :S5_STATE] = x_re[:, :S5_STATE]
        xim_ref[rows_f, :S5_STATE] = x_im[:, :S5_STATE]
        xre_ref[rows_b, S5_STATE:] = x_re[:, S5_STATE:]
        xim_ref[rows_b, S5_STATE:] = x_im[:, S5_STATE:]
        e_re = jnp.where(is_fwd, ere_ref[rows_f, :], ere_ref[rows_b, :])
        e_im = jnp.where(is_fwd, eim_ref[rows_f, :], eim_ref[rows_b, :])
        return (lam_re * x_re - lam_im * x_im + e_re, lam_re * x_im + lam_im * x_re + e_im)

    zero = jnp.zeros((n_groups, half), F32)
    lax.fori_loop(0, n_chunks, step, (zero, zero))

    for g in range(n_groups):
        xin = jnp.concatenate([xre_ref[pl.ds(g, n_chunks, stride=n_groups), :],
                               xim_ref[pl.ds(g, n_chunks, stride=n_groups), :]], axis=1)
        y_ref[0, g] = (jnp.dot(u_ref[0, g], win_ref[g], preferred_element_type=F32)
                       + _bdot(xin, wsi_ref[g])).astype(y_ref.dtype)


def _s5_call(u_t, w_so, w_in, w_si, lam_c, n_ctx_chunks, groups_per_step):
    batch, n_groups, n_chunks, width = u_t.shape
    gs = groups_per_step
    return pl.pallas_call(
        functools.partial(_s5_kernel, n_groups=gs, n_chunks=n_chunks, n_ctx_chunks=n_ctx_chunks),
        out_shape=jax.ShapeDtypeStruct(u_t.shape, BF16),
        grid=(batch, n_groups // gs),
        in_specs=[
            pl.BlockSpec((1, gs, n_chunks, width), lambda b, g: (b, g, 0, 0)),
            pl.BlockSpec((gs, width, width), lambda b, g: (g, 0, 0)),
            pl.BlockSpec((gs, width, width), lambda b, g: (g, 0, 0)),
            pl.BlockSpec((gs, width, width), lambda b, g: (g, 0, 0)),
            pl.BlockSpec((gs, width), lambda b, g: (g, 0)),
        ],
        out_specs=pl.BlockSpec((1, gs, n_chunks, width), lambda b, g: (b, g, 0, 0)),
        scratch_shapes=[pltpu.VMEM((n_chunks * gs, width // 2), F32) for _ in range(4)],
        compiler_params=_params("parallel", "parallel"),
        name="s5_mixer",
    )(u_t, w_so, w_in, w_si, lam_c)


def _s5_operators(lam_re, lam_im, log_step, b_re, b_im, c_re, c_im):
    cs = S5_CHUNK
    step = jnp.exp(log_step)[..., None]
    mag = jnp.exp(lam_re * step)
    lb_re = mag * jnp.cos(lam_im * step)
    lb_im = mag * jnp.sin(lam_im * step)
    den = lam_re * lam_re + lam_im * lam_im
    f_re = ((lb_re - 1.0) * lam_re + lb_im * lam_im) / den
    f_im = (lb_im * lam_re - (lb_re - 1.0) * lam_im) / den
    bb_re = f_re[..., None] * b_re - f_im[..., None] * b_im
    bb_im = f_re[..., None] * b_im + f_im[..., None] * b_re
    p_re, p_im = [jnp.ones_like(lb_re)], [jnp.zeros_like(lb_re)]
    for _ in range(cs):
        p_re, p_im = (p_re + [p_re[-1] * lb_re - p_im[-1] * lb_im],
                      p_im + [p_re[-1] * lb_im + p_im[-1] * lb_re])
    p_re = jnp.stack(p_re)
    p_im = jnp.stack(p_im)
    cp_re = c_re[None] * p_re[:, :, :, None, :] - c_im[None] * p_im[:, :, :, None, :]
    cp_im = c_re[None] * p_im[:, :, :, None, :] + c_im[None] * p_re[:, :, :, None, :]
    kern = (jnp.einsum("tdgop,dgpa->tdgoa", cp_re, bb_re, precision=HIGHEST)
            - jnp.einsum("tdgop,dgpa->tdgoa", cp_im, bb_im, precision=HIGHEST))
    pos = jnp.arange(cs)
    lag = pos[None, :] - pos[:, None]
    k_f = jnp.where((lag >= 0)[:, :, None, None, None], kern[jnp.clip(lag, 0, cs), 0], 0.0)
    k_b = jnp.where((lag <= 0)[:, :, None, None, None], kern[jnp.clip(-lag, 0, cs), 1], 0.0)
    n_g = lam_re.shape[1]
    w_in = jnp.transpose(k_f + k_b, (2, 0, 4, 1, 3)).reshape(n_g, cs * S5_GROUP, cs * S5_GROUP)

    def state_out(power_idx, d):
        pr = p_re[power_idx, d][..., None]
        pi = p_im[power_idx, d][..., None]
        re = pr * bb_re[d][None] - pi * bb_im[d][None]
        im = pr * bb_im[d][None] + pi * bb_re[d][None]
        shape = (n_g, cs * S5_GROUP, S5_STATE)
        return (jnp.transpose(re, (1, 0, 3, 2)).reshape(shape),
                jnp.transpose(im, (1, 0, 3, 2)).reshape(shape))

    of_re, of_im = state_out(cs - 1 - pos, 0)
    ob_re, ob_im = state_out(pos, 1)
    w_so = jnp.concatenate([of_re, ob_re, of_im, ob_im], axis=2)

    def state_in(power_idx, d):
        re = jnp.transpose(cp_re[power_idx, d], (1, 3, 0, 2))
        im = jnp.transpose(cp_im[power_idx, d], (1, 3, 0, 2))
        shape = (n_g, S5_STATE, cs * S5_GROUP)
        return re.reshape(shape), -im.reshape(shape)

    if_re, if_im = state_in(pos + 1, 0)
    ib_re, ib_im = state_in(cs - pos, 1)
    w_si = jnp.concatenate([if_re, ib_re, if_im, ib_im], axis=1)
    lam_c = jnp.concatenate([p_re[cs, 0], p_re[cs, 1], p_im[cs, 0], p_im[cs, 1]], axis=1)
    return w_so.astype(BF16), w_in.astype(BF16), w_si.astype(BF16), lam_c


def _outproj_kernel(x_ref, ga_ref, sa_ref, y5_ref, u_ref, d5_ref, wglu_ref, bglu_ref,
                    w0_ref, w1_ref, w2_ref, nw_ref, gate_ref, o_ref, *, nb):
    y5 = y5_ref[...].astype(F32) + u_ref[...].astype(F32) * d5_ref[...]
    ge = y5 * (0.5 * (1.0 + jnp.tanh(math.sqrt(2.0 / math.pi) * (y5 + 0.044715 * (y5 * y5 * y5)))))
    s5 = ge * _sigmoid(_bdot(ge, wglu_ref[...]) + bglu_ref[...])
    acc = _bdot(ga_ref[...], w0_ref[...]) + _bdot(sa_ref[...], w1_ref[...]) + _bdot(s5, w2_ref[...])
    r = _rms(acc, nw_ref[...])
    for s in range(nb):
        rows = slice(s * ROW_BLOCK, (s + 1) * ROW_BLOCK)
        o_ref[rows, :] = x_ref[rows, :] + gate_ref[s] * r[rows]


def _outproj_call(x, gdn_o, ssd_o, y5, h, d5, wglu, bglu, w_out, nw, modtab, tm):
    t, d = x.shape
    nb = tm // ROW_BLOCK
    u0 = COL_U // S5_DIM
    row = lambda i: (i, 0)
    const = lambda i: (0, 0)
    return pl.pallas_call(
        functools.partial(_outproj_kernel, nb=nb),
        out_shape=jax.ShapeDtypeStruct((t, d), F32),
        grid=(t // tm,),
        in_specs=[
            pl.BlockSpec((tm, d), row),
            pl.BlockSpec((tm, GDN_DIM), row),
            pl.BlockSpec((tm, SSM_DIM), row),
            pl.BlockSpec((tm, S5_DIM), row),
            pl.BlockSpec((tm, S5_DIM), lambda i: (i, u0)),
            pl.BlockSpec((1, S5_DIM), const),
            pl.BlockSpec((S5_DIM, S5_DIM), const),
            pl.BlockSpec((1, S5_DIM), const),
            pl.BlockSpec((GDN_DIM, d), const),
            pl.BlockSpec((SSM_DIM, d), lambda i: (1, 0)),
            pl.BlockSpec((S5_DIM, d), lambda i: ((GDN_DIM + SSM_DIM) // S5_DIM, 0)),
            pl.BlockSpec((1, d), const),
            pl.BlockSpec((nb, 1, d), lambda i: (i, 0, 2)),
        ],
        out_specs=pl.BlockSpec((tm, d), row),
        input_output_aliases={0: 0},
        compiler_params=_params("parallel"),
        name="out_proj",
    )(x, gdn_o, ssd_o, y5, h, d5, wglu, bglu, w_out, w_out, w_out, nw, modtab)


def _ffn_kernel(x_ref, nw_ref, sh_ref, sc_ref, wg_ref, wu_ref, wo_ref, nwo_ref, gate_ref, o_ref,
                xn_ref, acc_ref, *, nb):
    j = pl.program_id(1)

    @pl.when(j == 0)
    def _():
        y = _rms(x_ref[...], nw_ref[...])
        for s in range(nb):
            r = slice(s * ROW_BLOCK, (s + 1) * ROW_BLOCK)
            xn_ref[r, :] = (y[r] * (1.0 + sc_ref[s]) + sh_ref[s]).astype(BF16)
        acc_ref[...] = jnp.zeros(acc_ref.shape, F32)

    xn = xn_ref[...]
    gate = jnp.dot(xn, wg_ref[...], preferred_element_type=F32)
    up = jnp.dot(xn, wu_ref[...], preferred_element_type=F32)
    acc_ref[...] += _bdot(_silu(gate) * up, wo_ref[...])

    @pl.when(j == pl.num_programs(1) - 1)
    def _():
        r = _rms(acc_ref[...], nwo_ref[...])
        for s in range(nb):
            rows = slice(s * ROW_BLOCK, (s + 1) * ROW_BLOCK)
            o_ref[rows, :] = x_ref[rows, :] + gate_ref[s] * r[rows]


def _ffn_call(x, nw, nwo, modtab, w_in, w_out, tm, tf):
    t, d = x.shape
    f = w_out.shape[0]
    nb = tm // ROW_BLOCK
    nf = f // tf
    return pl.pallas_call(
        functools.partial(_ffn_kernel, nb=nb),
        out_shape=jax.ShapeDtypeStruct((t, d), F32),
        grid=(t // tm, nf),
        in_specs=[
            pl.BlockSpec((tm, d), lambda i, j: (i, 0)),
            pl.BlockSpec((1, d), lambda i, j: (0, 0)),
            pl.BlockSpec((nb, 1, d), lambda i, j: (i, 0, 3)),
            pl.BlockSpec((nb, 1, d), lambda i, j: (i, 0, 4)),
            pl.BlockSpec((d, tf), lambda i, j: (0, j)),
            pl.BlockSpec((d, tf), lambda i, j: (0, nf + j)),
            pl.BlockSpec((tf, d), lambda i, j: (j, 0)),
            pl.BlockSpec((1, d), lambda i, j: (0, 0)),
            pl.BlockSpec((nb, 1, d), lambda i, j: (i, 0, 5)),
        ],
        out_specs=pl.BlockSpec((tm, d), lambda i, j: (i, 0)),
        scratch_shapes=[pltpu.VMEM((tm, d), BF16), pltpu.VMEM((tm, d), F32)],
        input_output_aliases={0: 0},
        compiler_params=_params("parallel", "arbitrary"),
        name="swiglu_ffn",
    )(x, nw, modtab, modtab, w_in, w_in, w_out, nwo, modtab)


def _largest_tile(total, cap):
    tile = ROW_BLOCK
    while tile * 2 <= cap and total % (tile * 2) == 0:
        tile *= 2
    return tile


def kernel(x, c, ctx, c_ctx, w_mod, b_mod, norm_w, w_in, w_out, gdn_conv, gdn_a_log, gdn_dt_bias, gdn_norm, ssm_conv, ssm_a_log, ssm_dt_bias, ssm_d, ssm_norm, s5_lam_re, s5_lam_im, s5_log_step, s5_b_re, s5_b_im, s5_c_re, s5_c_im, s5_d, s5_w_glu, s5_b_glu, w_ffn_in, w_ffn_out):
    batch, seq, d = x.shape
    ctx_len = ctx.shape[1]
    depth = w_mod.shape[0]
    assert ctx_len == ROW_BLOCK and seq % ROW_BLOCK == 0 and d % LANES == 0
    lc = ctx_len + seq
    t = batch * lc
    blocks_per_seq = lc // ROW_BLOCK
    ctx_blocks = ctx_len // ROW_BLOCK
    n_blocks = t // ROW_BLOCK
    n_mix_chunks = lc // MIX_CHUNK
    n_s5_chunks = lc // S5_CHUNK

    xs = jnp.concatenate([ctx, x], axis=1).reshape(t, d)

    mod_rows = ((batch + 1 + 7) // 8) * 8
    cvec = jnp.zeros((mod_rows, d), F32).at[:batch].set(c).at[batch].set(c_ctx)
    mod = _mod_call(cvec, w_mod, b_mod)
    blk = jnp.arange(n_blocks)
    mod_row = jnp.where(blk % blocks_per_seq < ctx_blocks, batch, blk // blocks_per_seq)

    tm_in = _largest_tile(t, 1024)
    tm_out = ROW_BLOCK
    tm_ffn = _largest_tile(t, 512)

    o_qkv = 0
    o_gz = o_qkv + 3 * GDN_DIM
    o_ga = o_gz + GDN_DIM
    o_gb = o_ga + 2 * GDN_HEADS
    o_sz = o_gb + 2 * GDN_HEADS
    o_xbc = o_sz + SSM_DIM
    o_sdt = o_xbc + SSM_CONV_DIM
    o_u = o_sdt + 2 * SSM_HEADS

    for i in range(depth):
        modtab = mod[i][mod_row].reshape(n_blocks, 1, 6 * d)
        wi = w_in[i]
        w_main = jnp.concatenate([wi[:, o_qkv:o_ga], wi[:, o_sz:o_sdt], wi[:, o_u:o_u + S5_DIM]],
                                 axis=1).astype(BF16)
        w_gate = jnp.concatenate([wi[:, o_ga:o_sz], wi[:, o_sdt:o_u]], axis=1)
        w_gate = jnp.pad(w_gate, ((0, 0), (0, GATE_DIM - w_gate.shape[1]))).astype(BF16)
        h, hg, hgt = _inproj_call(xs, norm_w[i, 0][None], modtab, w_main, w_gate, w_gate.T,
                                  tm_in, 512)

        qkv = _prep_call(h, COL_QKV, GDN_DIM, 3, gdn_conv[i], 1, 1, GDN_HEAD_DIM ** -0.5,
                         blocks_per_seq, ctx_blocks)
        hh = jnp.arange(GDN_HEADS)
        gsel = jnp.stack([GATE_A + hh, GATE_A + GDN_HEADS + hh, GATE_B + hh, GATE_B + GDN_HEADS + hh], axis=1)
        gcol = jnp.pad(jnp.transpose(hg[:, gsel], (1, 0, 2)), ((0, 0), (0, 0), (0, 4)))
        grow = jnp.pad(hgt[gsel], ((0, 0), (0, 4), (0, 0)))
        grow = jnp.transpose(grow.reshape(GDN_HEADS, 8, t // MIX_CHUNK, MIX_CHUNK), (0, 2, 1, 3))
        hp4 = jnp.concatenate([gdn_a_log[i], gdn_dt_bias[i]], axis=0).T
        hp = jnp.broadcast_to(jnp.pad(hp4, ((0, 0), (0, 4)))[:, :, None], (GDN_HEADS, 8, LANES))
        hpc = jnp.zeros((GDN_HEADS, 8, 8), F32)
        hpc = hpc.at[:, 0, 0:2].set(gdn_a_log[i].T).at[:, 1, 0:2].set(gdn_dt_bias[i].T)
        gdn_o = _gdn_call(qkv, h, gcol, grow, hp, hpc, gdn_norm[i][None], batch, lc,
                          ctx_len // MIX_CHUNK)

        xbc = _prep_call(h, COL_XBC, SSM_CONV_DIM, 1, ssm_conv[i], 0, 0, 1.0, blocks_per_seq, ctx_blocks)
        gh = jnp.arange(SSM_GROUP_HEADS)
        dsel = jnp.stack([jnp.concatenate([GATE_DT + g * SSM_GROUP_HEADS + gh,
                                           GATE_DT + SSM_HEADS + g * SSM_GROUP_HEADS + gh])
                          for g in range(SSM_GROUPS)])
        dcol = jnp.pad(jnp.transpose(hg[:, dsel], (1, 0, 2)), ((0, 0), (0, 0), (0, 4)))
        drow = jnp.pad(hgt[dsel], ((0, 0), (0, 4), (0, 0)))
        drow = jnp.transpose(drow.reshape(SSM_GROUPS, 16, t // MIX_CHUNK, MIX_CHUNK), (0, 2, 1, 3))
        sp = jnp.stack([ssm_a_log[i].reshape(2, SSM_GROUPS, SSM_GROUP_HEADS),
                        ssm_dt_bias[i].reshape(2, SSM_GROUPS, SSM_GROUP_HEADS)])
        sp = jnp.transpose(sp, (2, 0, 1, 3)).reshape(SSM_GROUPS, 2, 2 * SSM_GROUP_HEADS)
        prow = jnp.pad(sp, ((0, 0), (0, 6), (0, 4)))
        pcol = jnp.pad(jnp.transpose(sp, (0, 2, 1)), ((0, 0), (0, 4), (0, 6)))
        dskip = jnp.repeat(ssm_d[i], SSM_HEAD_DIM).reshape(SSM_GROUPS, 1, SSM_GROUP_DIM)
        ssd_o = _ssd_call(xbc, h, dcol, drow, prow, pcol, dskip,
                          ssm_norm[i].reshape(SSM_GROUPS, 1, SSM_GROUP_DIM), batch, lc,
                          ctx_len // MIX_CHUNK)

        w_so, w_intra, w_si, lam_c = _s5_operators(s5_lam_re[i], s5_lam_im[i], s5_log_step[i],
                                                   s5_b_re[i], s5_b_im[i], s5_c_re[i], s5_c_im[i])
        u = h[:, COL_U:COL_U + S5_DIM].astype(BF16)
        u_t = jnp.transpose(u.reshape(batch, n_s5_chunks, S5_CHUNK, S5_GROUPS, S5_GROUP), (0, 3, 1, 2, 4))
        u_t = u_t.reshape(batch, S5_GROUPS, n_s5_chunks, S5_CHUNK * S5_GROUP)
        y_t = _s5_call(u_t, w_so, w_intra, w_si, lam_c, ctx_len // S5_CHUNK, 16)
        y5 = jnp.transpose(y_t.reshape(batch, S5_GROUPS, n_s5_chunks, S5_CHUNK, S5_GROUP), (0, 2, 3, 1, 4))
        y5 = y5.reshape(t, S5_DIM)

        xs = _outproj_call(xs, gdn_o, ssd_o, y5, h, s5_d[i][None], s5_w_glu[i].astype(BF16),
                           s5_b_glu[i][None], w_out[i].astype(BF16), norm_w[i, 1][None], modtab, tm_out)
        xs = _ffn_call(xs, norm_w[i, 2][None], norm_w[i, 3][None], modtab,
                       w_ffn_in[i].astype(BF16), w_ffn_out[i].astype(BF16), tm_ffn, 512)

    return xs.reshape(batch, lc, d)[:, ctx_len:]
```

```python
import functools
import math

import jax
import jax.numpy as jnp
from jax import lax
from jax.experimental import pallas as pl
from jax.experimental.pallas import tpu as pltpu

F32 = jnp.float32
BF16 = jnp.bfloat16
HIGHEST = lax.Precision.HIGHEST

RMS_EPS = 1e-6
GRID_W = 64
SHORT_CONV = 5

GDN_HEADS = 6
GDN_HEAD_DIM = 128
GDN_DIM = GDN_HEADS * GDN_HEAD_DIM
SSM_HEADS = 12
SSM_HEAD_DIM = 64
SSM_DIM = SSM_HEADS * SSM_HEAD_DIM
SSM_GROUPS = 2
SSM_STATE = 128
SSM_CONV_DIM = SSM_DIM + 2 * SSM_GROUPS * SSM_STATE
SSM_GROUP_HEADS = SSM_HEADS // SSM_GROUPS
SSM_GROUP_DIM = SSM_DIM // SSM_GROUPS
S5_GROUPS = 32
S5_GROUP = 16
S5_DIM = S5_GROUPS * S5_GROUP
S5_STATE = 64
MIX_DIM = GDN_DIM + SSM_DIM + S5_DIM

COL_QKV = 0
COL_GZ = 3 * GDN_DIM
COL_SZ = COL_GZ + GDN_DIM
COL_XBC = COL_SZ + SSM_DIM
COL_U = COL_XBC + SSM_CONV_DIM
MAIN_DIM = COL_U + S5_DIM
GATE_DIM = 128
GATE_A = 0
GATE_B = 2 * GDN_HEADS
GATE_DT = 4 * GDN_HEADS

LANES = 128
ROW_BLOCK = 256
MIX_CHUNK = 128
GDN_PREP_UNROLL = 2
S5_CHUNK = 16
VMEM_LIMIT = 56 * 1024 * 1024
NEG_BIG = -1e30


def _sigmoid(v):
    return 1.0 / (1.0 + jnp.exp(-v))


def _silu(v):
    return v * _sigmoid(v)


def _softplus(v):
    return jnp.maximum(v, 0.0) + jnp.log(1.0 + jnp.exp(-jnp.abs(v)))


def _rms(v, w):
    return v * lax.rsqrt(jnp.mean(v * v, axis=-1, keepdims=True) + RMS_EPS) * w


def _bdot(a, b):
    return jnp.dot(a.astype(BF16), b.astype(BF16), preferred_element_type=F32)


def _fdot(a, b):
    return jnp.dot(a, b, precision=HIGHEST, preferred_element_type=F32)


def _dot16(a, b):
    return jnp.dot(a, b, preferred_element_type=F32)


SOLVE_BASE = 8


def _split2(v):
    hi = v.astype(BF16)
    return hi, (v - hi.astype(F32)).astype(BF16)


def _dot3(a_parts, b_parts):
    (ah, al), (bh, bl) = a_parts, b_parts
    return _dot16(ah, bh) + _dot16(al, bh) + _dot16(ah, bl)


def _unit_triangular_solves(ms, ii, jj, rhss):
    n = ms[0].shape[0]

    def same_block(size):
        shift = int(math.log2(size))
        return jnp.right_shift(ii, shift) == jnp.right_shift(jj, shift)

    eye = (ii == jj).astype(F32)
    diag = [jnp.where(same_block(SOLVE_BASE), m, 0.0) for m in ms]
    invs = [eye - d for d in diag]
    powers = [_split2(d) for d in diag]
    for _ in range(int(math.log2(SOLVE_BASE)) - 1):
        powers = [_split2(_dot3(a, a)) for a in powers]
        invs = [inv + _dot3(_split2(inv), a) for inv, a in zip(invs, powers)]
    size = SOLVE_BASE
    while size < n:
        off_diag = jnp.logical_and(same_block(2 * size), jnp.logical_not(same_block(size)))
        inv_parts = [_split2(inv) for inv in invs]
        coupled = [_dot3(_split2(jnp.where(off_diag, m, 0.0)), ip) for m, ip in zip(ms, inv_parts)]
        invs = [inv - _dot3(ip, _split2(c)) for inv, ip, c in zip(invs, inv_parts, coupled)]
        size *= 2
    return [_dot3(_split2(inv), _split2(rhs)) for inv, rhs in zip(invs, rhss)]


def _params(*sem):
    return pltpu.CompilerParams(dimension_semantics=sem, vmem_limit_bytes=VMEM_LIMIT)


def _mod_kernel(c_ref, w_ref, b_ref, o_ref):
    o_ref[0] = _fdot(_silu(c_ref[...]), w_ref[0]) + b_ref[0]


def _mod_call(cvec, w_mod, b_mod):
    depth, d_model, n = w_mod.shape
    rows = cvec.shape[0]
    tn = 1024
    return pl.pallas_call(
        _mod_kernel,
        out_shape=jax.ShapeDtypeStruct((depth, rows, n), F32),
        grid=(depth, n // tn),
        in_specs=[
            pl.BlockSpec((rows, d_model), lambda l, j: (0, 0)),
            pl.BlockSpec((1, d_model, tn), lambda l, j: (l, 0, j)),
            pl.BlockSpec((1, 1, tn), lambda l, j: (l, 0, j)),
        ],
        out_specs=pl.BlockSpec((1, rows, tn), lambda l, j: (l, 0, j)),
        compiler_params=_params("parallel", "parallel"),
        name="adaln_mod",
    )(cvec, w_mod, b_mod.reshape(depth, 1, n))


def _inproj_kernel(x_ref, nw_ref, sh_ref, sc_ref, w_ref, wg_ref, wgt_ref,
                   h_ref, hg_ref, hgt_ref, xn_ref, *, nb):
    @pl.when(pl.program_id(1) == 0)
    def _():
        y = _rms(x_ref[...], nw_ref[...])
        for s in range(nb):
            r = slice(s * ROW_BLOCK, (s + 1) * ROW_BLOCK)
            xn_ref[r, :] = (y[r] * (1.0 + sc_ref[s]) + sh_ref[s]).astype(BF16)
        xn = xn_ref[...]
        hg_ref[...] = jnp.dot(xn, wg_ref[...], preferred_element_type=F32)
        hgt_ref[...] = lax.dot_general(wgt_ref[...], xn, (((1,), (1,)), ((), ())),
                                       preferred_element_type=F32)

    h_ref[...] = jnp.dot(xn_ref[...], w_ref[...], preferred_element_type=F32).astype(h_ref.dtype)


def _inproj_call(x, nw, modtab, w_main, w_gate, w_gate_t, tm, tn):
    t, d = x.shape
    n = w_main.shape[1]
    nb = tm // ROW_BLOCK
    return pl.pallas_call(
        functools.partial(_inproj_kernel, nb=nb),
        out_shape=(jax.ShapeDtypeStruct((t, n), BF16),
                   jax.ShapeDtypeStruct((t, GATE_DIM), F32),
                   jax.ShapeDtypeStruct((GATE_DIM, t), F32)),
        grid=(t // tm, n // tn),
        in_specs=[
            pl.BlockSpec((tm, d), lambda i, j: (i, 0)),
            pl.BlockSpec((1, d), lambda i, j: (0, 0)),
            pl.BlockSpec((nb, 1, d), lambda i, j: (i, 0, 0)),
            pl.BlockSpec((nb, 1, d), lambda i, j: (i, 0, 1)),
            pl.BlockSpec((d, tn), lambda i, j: (0, j)),
            pl.BlockSpec((d, GATE_DIM), lambda i, j: (0, 0)),
            pl.BlockSpec((GATE_DIM, d), lambda i, j: (0, 0)),
        ],
        out_specs=(pl.BlockSpec((tm, tn), lambda i, j: (i, j)),
                   pl.BlockSpec((tm, GATE_DIM), lambda i, j: (i, 0)),
                   pl.BlockSpec((GATE_DIM, tm), lambda i, j: (0, i))),
        scratch_shapes=[pltpu.VMEM((tm, d), BF16)],
        compiler_params=_params("parallel", "arbitrary"),
        name="in_proj",
    )(x, nw, modtab, modtab, w_main, w_gate, w_gate_t)


def _prep_kernel(h_ref, w_ref, o_ref, *, n_q, n_k, q_scale, blocks_per_seq, ctx_blocks):
    i = pl.program_id(0)
    j = pl.program_id(1)
    x = h_ref[...].astype(F32)
    width = x.shape[1]
    t = lax.broadcasted_iota(jnp.int32, (ROW_BLOCK, 1), 0)
    is_ctx = (i % blocks_per_seq) < ctx_blocks
    seg = jnp.where(is_ctx, ROW_BLOCK, GRID_W)
    pos = jnp.bitwise_and(t, seg - 1)
    half = SHORT_CONV // 2
    acc = x * w_ref[half:half + 1, :]
    for dlt in range(-half, half + 1):
        if dlt == 0:
            continue
        shifted = pltpu.roll(x, (-dlt) % ROW_BLOCK, axis=0)
        ok = jnp.logical_and(pos + dlt >= 0, pos + dlt < seg)
        acc = acc + jnp.where(ok, shifted, 0.0) * w_ref[dlt + half:dlt + half + 1, :]
    y = _silu(acc)
    norm_scale = jnp.where(j < n_q, q_scale, 1.0)
    use_norm = j < n_q + n_k
    for b in range(width // LANES):
        c = slice(b * LANES, (b + 1) * LANES)
        yb = y[:, c]
        inv = lax.rsqrt(jnp.sum(yb * yb, axis=-1, keepdims=True) + 1e-6) * norm_scale
        o_ref[:, c] = yb * jnp.where(use_norm, inv, 1.0)


def _prep_call(h, col0, width, n_col_blocks, conv_w, n_q, n_k, q_scale, blocks_per_seq, ctx_blocks):
    t = h.shape[0]
    cw = jnp.zeros((8, conv_w.shape[1]), F32).at[:SHORT_CONV].set(conv_w)
    cb0 = col0 // width
    return pl.pallas_call(
        functools.partial(_prep_kernel, n_q=n_q, n_k=n_k, q_scale=q_scale,
                          blocks_per_seq=blocks_per_seq, ctx_blocks=ctx_blocks),
        out_shape=jax.ShapeDtypeStruct((t, width * n_col_blocks), F32),
        grid=(t // ROW_BLOCK, n_col_blocks),
        in_specs=[
            pl.BlockSpec((ROW_BLOCK, width), lambda i, j: (i, cb0 + j)),
            pl.BlockSpec((8, width), lambda i, j: (0, j)),
        ],
        out_specs=pl.BlockSpec((ROW_BLOCK, width), lambda i, j: (i, j)),
        compiler_params=_params("parallel", "parallel"),
        name="conv_prep",
    )(h, cw)


def _bwd_chunk(s, n_chunks, n_ctx_chunks):
    return jnp.where(s < n_ctx_chunks, n_ctx_chunks - 1 - s, n_chunks + n_ctx_chunks - 1 - s)


def _tri_masks(c):
    ii = lax.broadcasted_iota(jnp.int32, (c, c), 0)
    jj = lax.broadcasted_iota(jnp.int32, (c, c), 1)
    return ii, jj


def _gdn_kernel(q_ref, k_ref, v_ref, z_ref, gc_ref, gr_ref, hp_ref, hpc_ref, nw_ref, o_ref,
                u_ref, wq_ref, qk_ref, kdt_ref, gt_ref, s_ref, *, n_chunks, n_ctx_chunks):
    c_len = MIX_CHUNK
    hd = GDN_HEAD_DIM
    ii, jj = _tri_masks(c_len)
    lower = (ii >= jj).astype(F32)
    upper = (ii <= jj).astype(F32)

    neg_a_row = -jnp.exp(hp_ref[0, 0:2, :])
    bias_row = hp_ref[0, 2:4, :]
    neg_a_col = -jnp.exp(hpc_ref[0, 0:1, 0:2])
    bias_col = hpc_ref[0, 1:2, 0:2]

    def prepare(i, carry):
        chunks = [i * GDN_PREP_UNROLL + k for k in range(GDN_PREP_UNROLL)]
        rows = [pl.ds(pl.multiple_of(c * c_len, c_len), c_len) for c in chunks]
        qs = [q_ref[r, :] for r in rows]
        ks = [k_ref[r, :] for r in rows]
        vs = [v_ref[r, :] for r in rows]
        kts = [kc.T for kc in ks]
        ktbs = [ktc.astype(BF16) for ktc in kts]
        kks = [_dot16(kc.astype(BF16), ktb) for kc, ktb in zip(ks, ktbs)]
        qkts = [_dot16(qc.astype(BF16), ktb) for qc, ktb in zip(qs, ktbs)]
        g_cols = [neg_a_col * _softplus(gc_ref[0, r, 0:2] + bias_col) for r in rows]
        betas = [_sigmoid(gc_ref[0, r, 2:4]) for r in rows]
        g_rows = [neg_a_row * _softplus(gr_ref[0, c][0:2, :] + bias_row) for c in chunks]
        probs = [(k, d) for k in range(GDN_PREP_UNROLL) for d in range(2)]
        cum_cols = [_fdot(lower if d == 0 else upper,
                          jnp.broadcast_to(g_cols[k][:, d:d + 1], (c_len, LANES))) for k, d in probs]
        cum_rows = [_fdot(jnp.broadcast_to(g_rows[k][d:d + 1], (8, c_len)),
                          upper if d == 0 else lower)[0:1, :] for k, d in probs]
        gammas = [jnp.exp(jnp.where((ii >= jj) if d == 0 else (ii <= jj), cc - cr, NEG_BIG))
                  for (k, d), cc, cr in zip(probs, cum_cols, cum_rows)]
        ms = [jnp.where((ii > jj) if d == 0 else (ii < jj), kks[k] * gm, 0.0) * betas[k][:, d:d + 1]
              for (k, d), gm in zip(probs, gammas)]
        ecums = [jnp.exp(cc) for cc in cum_cols]
        rhss = [jnp.concatenate([vs[k] * betas[k][:, d:d + 1], ks[k] * (betas[k][:, d:d + 1] * ec)], axis=1)
                for (k, d), ec in zip(probs, ecums)]
        sols = _unit_triangular_solves(ms, ii, jj, rhss)
        for (k, d), sol, gm, ec, cr in zip(probs, sols, gammas, ecums, cum_rows):
            c = chunks[k]
            cend = cr[:, c_len - 1:c_len] if d == 0 else cr[:, 0:1]
            u_ref[d, rows[k], :] = sol[:, :hd]
            wq_ref[d, c, 0:c_len, :] = sol[:, hd:].astype(BF16)
            wq_ref[d, c, c_len:2 * c_len, :] = (qs[k] * ec).astype(BF16)
            qk_ref[d, c] = (qkts[k] * gm).astype(BF16)
            kdt_ref[d, c] = (kts[k] * jnp.exp(cend - cr)).astype(BF16)
            gt_ref[d, c] = jnp.broadcast_to(jnp.exp(cend), (8, LANES))
        return carry

    lax.fori_loop(0, n_chunks // GDN_PREP_UNROLL, prepare, 0)
    s_ref[...] = jnp.zeros(s_ref.shape, F32)
    o_ref[...] = jnp.zeros(o_ref.shape, F32)

    def advance(d, c):
        rows = pl.ds(pl.multiple_of(c * c_len, c_len), c_len)
        state = s_ref[d]
        ws_qs = _dot16(wq_ref[d, c], state.astype(BF16))
        v_new = (u_ref[d, rows, :] - ws_qs[:c_len]).astype(BF16)
        o_ref[rows, :] += ws_qs[c_len:] + _dot16(qk_ref[d, c], v_new)
        s_ref[d] = state * gt_ref[d, c][0:1, :] + _dot16(kdt_ref[d, c], v_new)

    def step(s, carry):
        advance(0, s)
        advance(1, _bwd_chunk(s, n_chunks, n_ctx_chunks))
        return carry

    lax.fori_loop(0, n_chunks, step, 0)

    def finish(c, carry):
        rows = pl.ds(pl.multiple_of(c * c_len, c_len), c_len)
        o_ref[rows, :] = _rms(o_ref[rows, :], nw_ref[...]) * _silu(z_ref[rows, :].astype(F32))
        return carry

    lax.fori_loop(0, n_chunks, finish, 0)


def _gdn_call(qkv, h, gcol, grow, hp, hpc, nw, batch, lc, n_ctx_chunks):
    t = qkv.shape[0]
    n_chunks = lc // MIX_CHUNK
    hd = GDN_HEAD_DIM
    z0 = COL_GZ // hd
    return pl.pallas_call(
        functools.partial(_gdn_kernel, n_chunks=n_chunks, n_ctx_chunks=n_ctx_chunks),
        out_shape=jax.ShapeDtypeStruct((t, GDN_DIM), F32),
        grid=(batch, GDN_HEADS),
        in_specs=[
            pl.BlockSpec((lc, hd), lambda b, hh: (b, hh)),
            pl.BlockSpec((lc, hd), lambda b, hh: (b, GDN_HEADS + hh)),
            pl.BlockSpec((lc, hd), lambda b, hh: (b, 2 * GDN_HEADS + hh)),
            pl.BlockSpec((lc, hd), lambda b, hh: (b, z0 + hh)),
            pl.BlockSpec((1, lc, 8), lambda b, hh: (hh, b, 0)),
            pl.BlockSpec((1, n_chunks, 8, MIX_CHUNK), lambda b, hh: (hh, b, 0, 0)),
            pl.BlockSpec((1, 8, LANES), lambda b, hh: (hh, 0, 0)),
            pl.BlockSpec((1, 8, 8), lambda b, hh: (hh, 0, 0)),
            pl.BlockSpec((1, hd), lambda b, hh: (0, 0)),
        ],
        out_specs=pl.BlockSpec((lc, hd), lambda b, hh: (b, hh)),
        scratch_shapes=[pltpu.VMEM((2, lc, hd), F32),
                        pltpu.VMEM((2, n_chunks, 2 * MIX_CHUNK, hd), BF16),
                        pltpu.VMEM((2, n_chunks, MIX_CHUNK, MIX_CHUNK), BF16),
                        pltpu.VMEM((2, n_chunks, hd, MIX_CHUNK), BF16),
                        pltpu.VMEM((2, n_chunks, 8, LANES), F32),
                        pltpu.VMEM((2, hd, hd), F32)],
        compiler_params=_params("parallel", "parallel"),
        name="gdn_mixer",
    )(qkv, qkv, qkv, h, gcol, grow, hp, hpc, nw)


def _ssd_kernel(x_ref, b_ref, c_ref, z_ref, dc_ref, dr_ref, pr_ref, pc_ref, dsk_ref, nw_ref, o_ref,
                bt_ref, cb_ref, st_ref, *, n_chunks, n_ctx_chunks):
    c_len = MIX_CHUNK
    hg = SSM_GROUP_HEADS
    hp = SSM_HEAD_DIM
    ii, jj = _tri_masks(c_len)
    lower = (ii >= jj).astype(F32)
    upper = (ii <= jj).astype(F32)

    def prepare(c, carry):
        rows = pl.ds(pl.multiple_of(c * c_len, c_len), c_len)
        btc = b_ref[rows, :].T
        bt_ref[c] = btc
        cb_ref[c] = _bdot(c_ref[rows, :], btc)
        return carry

    lax.fori_loop(0, n_chunks, prepare, 0)
    st_ref[...] = jnp.zeros(st_ref.shape, F32)
    o_ref[...] = jnp.zeros(o_ref.shape, F32)

    neg_a_row = -jnp.exp(pr_ref[0, 0:1, :])
    bias_row = pr_ref[0, 1:2, :]
    neg_a_col = -jnp.exp(pc_ref[0, :, 0:1])
    bias_col = pc_ref[0, :, 1:2]

    def step(s, carry):
        dirs = range(2)
        chunks = [s, _bwd_chunk(s, n_chunks, n_ctx_chunks)]
        rows = [pl.ds(pl.multiple_of(c * c_len, c_len), c_len) for c in chunks]
        xcs = [x_ref[r, :] for r in rows]
        dtcs = [_softplus(dc_ref[0, r, :] + bias_row) for r in rows]
        dtrs = [_softplus(dr_ref[0, c] + bias_col) for c in chunks]
        cum_cols = [_fdot(lower if d == 0 else upper, dtcs[d] * neg_a_row) for d in dirs]
        cum_rows = [_fdot(dtrs[d] * neg_a_col, upper if d == 0 else lower) for d in dirs]
        states = [st_ref[d] for d in dirs]
        y_offs = [_bdot(c_ref[rows[d], :], states[d]) for d in dirs]
        cbs = [cb_ref[c] for c in chunks]
        cends = [cum_cols[0][c_len - 1:c_len, :], cum_cols[1][0:1, :]]
        to_ends = [jnp.exp(cends[d] - cum_cols[d]) for d in dirs]
        ecums = [jnp.exp(cum_cols[d]) for d in dirs]
        gtots = [jnp.exp(cends[d]) for d in dirs]
        heads = [(d, hh) for d in dirs for hh in range(hg)]
        decays = [jnp.exp(jnp.where((ii >= jj) if d == 0 else (ii <= jj),
                                    cum_cols[d][:, d * hg + hh:d * hg + hh + 1]
                                    - cum_rows[d][d * hg + hh:d * hg + hh + 1, :], NEG_BIG))
                  for d, hh in heads]
        xdts = [xcs[d][:, hh * hp:(hh + 1) * hp] * dtcs[d][:, d * hg + hh:d * hg + hh + 1] for d, hh in heads]
        y_diags = [_bdot(cbs[d] * dec, xdt) for (d, hh), dec, xdt in zip(heads, decays, xdts)]
        for d in dirs:
            ys, xds, gts = [], [], []
            for hh in range(hg):
                col = d * hg + hh
                k = d * hg + hh
                ys.append(y_diags[k] + ecums[d][:, col:col + 1] * y_offs[d][:, hh * hp:(hh + 1) * hp])
                xds.append(xdts[k] * to_ends[d][:, col:col + 1])
                gts.append(jnp.broadcast_to(gtots[d][:, col:col + 1], (1, hp)))
            st_ref[d] = (states[d] * jnp.concatenate(gts, axis=1)
                         + _bdot(bt_ref[chunks[d]], jnp.concatenate(xds, axis=1)))
            o_ref[rows[d], :] += jnp.concatenate(ys, axis=1)
        return carry

    lax.fori_loop(0, n_chunks, step, 0)

    def finish(c, carry):
        rows = pl.ds(pl.multiple_of(c * c_len, c_len), c_len)
        y = o_ref[rows, :] + x_ref[rows, :] * dsk_ref[0]
        o_ref[rows, :] = _rms(y * _silu(z_ref[rows, :].astype(F32)), nw_ref[0])
        return carry

    lax.fori_loop(0, n_chunks, finish, 0)


def _ssd_call(xbc, h, dcol, drow, prow, pcol, dskip, nw, batch, lc, n_ctx_chunks):
    t = xbc.shape[0]
    n_chunks = lc // MIX_CHUNK
    gd = SSM_GROUP_DIM
    ns = SSM_STATE
    b0 = SSM_DIM // ns
    c0 = b0 + SSM_GROUPS
    z0 = COL_SZ // gd
    return pl.pallas_call(
        functools.partial(_ssd_kernel, n_chunks=n_chunks, n_ctx_chunks=n_ctx_chunks),
        out_shape=jax.ShapeDtypeStruct((t, SSM_DIM), F32),
        grid=(batch, SSM_GROUPS),
        in_specs=[
            pl.BlockSpec((lc, gd), lambda b, g: (b, g), pipeline_mode=pl.Buffered(1)),
            pl.BlockSpec((lc, ns), lambda b, g: (b, b0 + g), pipeline_mode=pl.Buffered(1)),
            pl.BlockSpec((lc, ns), lambda b, g: (b, c0 + g), pipeline_mode=pl.Buffered(1)),
            pl.BlockSpec((lc, gd), lambda b, g: (b, z0 + g), pipeline_mode=pl.Buffered(1)),
            pl.BlockSpec((1, lc, 16), lambda b, g: (g, b, 0), pipeline_mode=pl.Buffered(1)),
            pl.BlockSpec((1, n_chunks, 16, MIX_CHUNK), lambda b, g: (g, b, 0, 0)),
            pl.BlockSpec((1, 8, 16), lambda b, g: (g, 0, 0)),
            pl.BlockSpec((1, 16, 8), lambda b, g: (g, 0, 0)),
            pl.BlockSpec((1, 1, gd), lambda b, g: (g, 0, 0)),
            pl.BlockSpec((1, 1, gd), lambda b, g: (g, 0, 0)),
        ],
        out_specs=pl.BlockSpec((lc, gd), lambda b, g: (b, g)),
        scratch_shapes=[pltpu.VMEM((n_chunks, ns, MIX_CHUNK), F32),
                        pltpu.VMEM((n_chunks, MIX_CHUNK, MIX_CHUNK), F32),
                        pltpu.VMEM((2, ns, gd), F32)],
        compiler_params=_params("parallel", "parallel"),
        name="ssd_mixer",
    )(xbc, xbc, xbc, h, dcol, drow, prow, pcol, dskip, nw)


def _s5_kernel(u_ref, wso_ref, win_ref, wsi_ref, lam_ref, y_ref, ere_ref, eim_ref, xre_ref, xim_ref, *,
               n_groups, n_chunks, n_ctx_chunks):
    half = 2 * S5_STATE
    for g in range(n_groups):
        e = jnp.dot(u_ref[0, g], wso_ref[g], preferred_element_type=F32)
        ere_ref[pl.ds(g, n_chunks, stride=n_groups), :] = e[:, :half]
        eim_ref[pl.ds(g, n_chunks, stride=n_groups), :] = e[:, half:]

    lam_re = lam_ref[:, :half]
    lam_im = lam_ref[:, half:]
    is_fwd = lax.broadcasted_iota(jnp.int32, (n_groups, half), 1) < S5_STATE

    def step(s, carry):
        x_re, x_im = carry
        cf = pl.multiple_of(s * n_groups, n_groups)
        cb = pl.multiple_of(_bwd_chunk(s, n_chunks, n_ctx_chunks) * n_groups, n_groups)
        rows_f = pl.ds(cf, n_groups)
        rows_b = pl.ds(cb, n_groups)
        xre_ref[rows_f, :S5_STATE] = x_re[:, :S5_STATE]
        xim_ref[rows_f, :S5_STATE] = x_im[:, :S5_STATE]
        xre_ref[rows_b, S5_STATE:] = x_re[:, S5_STATE:]
        xim_ref[rows_b, S5_STATE:] = x_im[:, S5_STATE:]
        e_re = jnp.where(is_fwd, ere_ref[rows_f, :], ere_ref[rows_b, :])
        e_im = jnp.where(is_fwd, eim_ref[rows_f, :], eim_ref[rows_b, :])
        return (lam_re * x_re - lam_im * x_im + e_re, lam_re * x_im + lam_im * x_re + e_im)

    zero = jnp.zeros((n_groups, half), F32)
    lax.fori_loop(0, n_chunks, step, (zero, zero))

    for g in range(n_groups):
        xin = jnp.concatenate([xre_ref[pl.ds(g, n_chunks, stride=n_groups), :],
                               xim_ref[pl.ds(g, n_chunks, stride=n_groups), :]], axis=1)
        y_ref[0, g] = (jnp.dot(u_ref[0, g], win_ref[g], preferred_element_type=F32)
                       + _bdot(xin, wsi_ref[g])).astype(y_ref.dtype)


def _s5_call(u_t, w_so, w_in, w_si, lam_c, n_ctx_chunks, groups_per_step):
    batch, n_groups, n_chunks, width = u_t.shape
    gs = groups_per_step
    return pl.pallas_call(
        functools.partial(_s5_kernel, n_groups=gs, n_chunks=n_chunks, n_ctx_chunks=n_ctx_chunks),
        out_shape=jax.ShapeDtypeStruct(u_t.shape, BF16),
        grid=(batch, n_groups // gs),
        in_specs=[
            pl.BlockSpec((1, gs, n_chunks, width), lambda b, g: (b, g, 0, 0)),
            pl.BlockSpec((gs, width, width), lambda b, g: (g, 0, 0)),
            pl.BlockSpec((gs, width, width), lambda b, g: (g, 0, 0)),
            pl.BlockSpec((gs, width, width), lambda b, g: (g, 0, 0)),
            pl.BlockSpec((gs, width), lambda b, g: (g, 0)),
        ],
        out_specs=pl.BlockSpec((1, gs, n_chunks, width), lambda b, g: (b, g, 0, 0)),
        scratch_shapes=[pltpu.VMEM((n_chunks * gs, width // 2), F32) for _ in range(4)],
        compiler_params=_params("parallel", "parallel"),
        name="s5_mixer",
    )(u_t, w_so, w_in, w_si, lam_c)


def _s5_operators(lam_re, lam_im, log_step, b_re, b_im, c_re, c_im):
    cs = S5_CHUNK
    step = jnp.exp(log_step)[..., None]
    mag = jnp.exp(lam_re * step)
    lb_re = mag * jnp.cos(lam_im * step)
    lb_im = mag * jnp.sin(lam_im * step)
    den = lam_re * lam_re + lam_im * lam_im
    f_re = ((lb_re - 1.0) * lam_re + lb_im * lam_im) / den
    f_im = (lb_im * lam_re - (lb_re - 1.0) * lam_im) / den
    bb_re = f_re[..., None] * b_re - f_im[..., None] * b_im
    bb_im = f_re[..., None] * b_im + f_im[..., None] * b_re
    p_re, p_im = [jnp.ones_like(lb_re)], [jnp.zeros_like(lb_re)]
    for _ in range(cs):
        p_re, p_im = (p_re + [p_re[-1] * lb_re - p_im[-1] * lb_im],
                      p_im + [p_re[-1] * lb_im + p_im[-1] * lb_re])
    p_re = jnp.stack(p_re)
    p_im = jnp.stack(p_im)
    cp_re = c_re[None] * p_re[:, :, :, None, :] - c_im[None] * p_im[:, :, :, None, :]
    cp_im = c_re[None] * p_im[:, :, :, None, :] + c_im[None] * p_re[:, :, :, None, :]
    kern = (jnp.einsum("tdgop,dgpa->tdgoa", cp_re, bb_re, precision=HIGHEST)
            - jnp.einsum("tdgop,dgpa->tdgoa", cp_im, bb_im, precision=HIGHEST))
    pos = jnp.arange(cs)
    lag = pos[None, :] - pos[:, None]
    k_f = jnp.where((lag >= 0)[:, :, None, None, None], kern[jnp.clip(lag, 0, cs), 0], 0.0)
    k_b = jnp.where((lag <= 0)[:, :, None, None, None], kern[jnp.clip(-lag, 0, cs), 1], 0.0)
    n_g = lam_re.shape[1]
    w_in = jnp.transpose(k_f + k_b, (2, 0, 4, 1, 3)).reshape(n_g, cs * S5_GROUP, cs * S5_GROUP)

    def state_out(power_idx, d):
        pr = p_re[power_idx, d][..., None]
        pi = p_im[power_idx, d][..., None]
        re = pr * bb_re[d][None] - pi * bb_im[d][None]
        im = pr * bb_im[d][None] + pi * bb_re[d][None]
        shape = (n_g, cs * S5_GROUP, S5_STATE)
        return (jnp.transpose(re, (1, 0, 3, 2)).reshape(shape),
                jnp.transpose(im, (1, 0, 3, 2)).reshape(shape))

    of_re, of_im = state_out(cs - 1 - pos, 0)
    ob_re, ob_im = state_out(pos, 1)
    w_so = jnp.concatenate([of_re, ob_re, of_im, ob_im], axis=2)

    def state_in(power_idx, d):
        re = jnp.transpose(cp_re[power_idx, d], (1, 3, 0, 2))
        im = jnp.transpose(cp_im[power_idx, d], (1, 3, 0, 2))
        shape = (n_g, S5_STATE, cs * S5_GROUP)
        return re.reshape(shape), -im.reshape(shape)

    if_re, if_im = state_in(pos + 1, 0)
    ib_re, ib_im = state_in(cs - pos, 1)
    w_si = jnp.concatenate([if_re, ib_re, if_im, ib_im], axis=1)
    lam_c = jnp.concatenate([p_re[cs, 0], p_re[cs, 1], p_im[cs, 0], p_im[cs, 1]], axis=1)
    return w_so.astype(BF16), w_in.astype(BF16), w_si.astype(BF16), lam_c


def _outproj_kernel(x_ref, ga_ref, sa_ref, y5_ref, u_ref, d5_ref, wglu_ref, bglu_ref,
                    w0_ref, w1_ref, w2_ref, nw_ref, gate_ref, o_ref, *, nb):
    y5 = y5_ref[...].astype(F32) + u_ref[...].astype(F32) * d5_ref[...]
    ge = y5 * (0.5 * (1.0 + jnp.tanh(math.sqrt(2.0 / math.pi) * (y5 + 0.044715 * (y5 * y5 * y5)))))
    s5 = ge * _sigmoid(_bdot(ge, wglu_ref[...]) + bglu_ref[...])
    acc = _bdot(ga_ref[...], w0_ref[...]) + _bdot(sa_ref[...], w1_ref[...]) + _bdot(s5, w2_ref[...])
    r = _rms(acc, nw_ref[...])
    for s in range(nb):
        rows = slice(s * ROW_BLOCK, (s + 1) * ROW_BLOCK)
        o_ref[rows, :] = x_ref[rows, :] + gate_ref[s] * r[rows]


def _outproj_call(x, gdn_o, ssd_o, y5, h, d5, wglu, bglu, w_out, nw, modtab, tm):
    t, d = x.shape
    nb = tm // ROW_BLOCK
    u0 = COL_U // S5_DIM
    row = lambda i: (i, 0)
    const = lambda i: (0, 0)
    return pl.pallas_call(
        functools.partial(_outproj_kernel, nb=nb),
        out_shape=jax.ShapeDtypeStruct((t, d), F32),
        grid=(t // tm,),
        in_specs=[
            pl.BlockSpec((tm, d), row),
            pl.BlockSpec((tm, GDN_DIM), row),
            pl.BlockSpec((tm, SSM_DIM), row),
            pl.BlockSpec((tm, S5_DIM), row),
            pl.BlockSpec((tm, S5_DIM), lambda i: (i, u0)),
            pl.BlockSpec((1, S5_DIM), const),
            pl.BlockSpec((S5_DIM, S5_DIM), const),
            pl.BlockSpec((1, S5_DIM), const),
            pl.BlockSpec((GDN_DIM, d), const),
            pl.BlockSpec((SSM_DIM, d), lambda i: (1, 0)),
            pl.BlockSpec((S5_DIM, d), lambda i: ((GDN_DIM + SSM_DIM) // S5_DIM, 0)),
            pl.BlockSpec((1, d), const),
            pl.BlockSpec((nb, 1, d), lambda i: (i, 0, 2)),
        ],
        out_specs=pl.BlockSpec((tm, d), row),
        input_output_aliases={0: 0},
        compiler_params=_params("parallel"),
        name="out_proj",
    )(x, gdn_o, ssd_o, y5, h, d5, wglu, bglu, w_out, w_out, w_out, nw, modtab)


def _ffn_kernel(x_ref, nw_ref, sh_ref, sc_ref, wg_ref, wu_ref, wo_ref, nwo_ref, gate_ref, o_ref,
                xn_ref, acc_ref, *, nb):
    j = pl.program_id(1)

    @pl.when(j == 0)
    def _():
        y = _rms(x_ref[...], nw_ref[...])
        for s in range(nb):
            r = slice(s * ROW_BLOCK, (s + 1) * ROW_BLOCK)
            xn_ref[r, :] = (y[r] * (1.0 + sc_ref[s]) + sh_ref[s]).astype(BF16)
        acc_ref[...] = jnp.zeros(acc_ref.shape, F32)

    xn = xn_ref[...]
    gate = jnp.dot(xn, wg_ref[...], preferred_element_type=F32)
    up = jnp.dot(xn, wu_ref[...], preferred_element_type=F32)
    acc_ref[...] += _bdot(_silu(gate) * up, wo_ref[...])

    @pl.when(j == pl.num_programs(1) - 1)
    def _():
        r = _rms(acc_ref[...], nwo_ref[...])
        for s in range(nb):
            rows = slice(s * ROW_BLOCK, (s + 1) * ROW_BLOCK)
            o_ref[rows, :] = x_ref[rows, :] + gate_ref[s] * r[rows]


def _ffn_call(x, nw, nwo, modtab, w_in, w_out, tm, tf):
    t, d = x.shape
    f = w_out.shape[0]
    nb = tm // ROW_BLOCK
    nf = f // tf
    return pl.pallas_call(
        functools.partial(_ffn_kernel, nb=nb),
        out_shape=jax.ShapeDtypeStruct((t, d), F32),
        grid=(t // tm, nf),
        in_specs=[
            pl.BlockSpec((tm, d), lambda i, j: (i, 0)),
            pl.BlockSpec((1, d), lambda i, j: (0, 0)),
            pl.BlockSpec((nb, 1, d), lambda i, j: (i, 0, 3)),
            pl.BlockSpec((nb, 1, d), lambda i, j: (i, 0, 4)),
            pl.BlockSpec((d, tf), lambda i, j: (0, j)),
            pl.BlockSpec((d, tf), lambda i, j: (0, nf + j)),
            pl.BlockSpec((tf, d), lambda i, j: (j, 0)),
            pl.BlockSpec((1, d), lambda i, j: (0, 0)),
            pl.BlockSpec((nb, 1, d), lambda i, j: (i, 0, 5)),
        ],
        out_specs=pl.BlockSpec((tm, d), lambda i, j: (i, 0)),
        scratch_shapes=[pltpu.VMEM((tm, d), BF16), pltpu.VMEM((tm, d), F32)],
        input_output_aliases={0: 0},
        compiler_params=_params("parallel", "arbitrary"),
        name="swiglu_ffn",
    )(x, nw, modtab, modtab, w_in, w_in, w_out, nwo, modtab)


def _largest_tile(total, cap):
    tile = ROW_BLOCK
    while tile * 2 <= cap and total % (tile * 2) == 0:
        tile *= 2
    return tile


def kernel(x, c, ctx, c_ctx, w_mod, b_mod, norm_w, w_in, w_out, gdn_conv, gdn_a_log, gdn_dt_bias, gdn_norm, ssm_conv, ssm_a_log, ssm_dt_bias, ssm_d, ssm_norm, s5_lam_re, s5_lam_im, s5_log_step, s5_b_re, s5_b_im, s5_c_re, s5_c_im, s5_d, s5_w_glu, s5_b_glu, w_ffn_in, w_ffn_out):
    batch, seq, d = x.shape
    ctx_len = ctx.shape[1]
    depth = w_mod.shape[0]
    assert ctx_len == ROW_BLOCK and seq % ROW_BLOCK == 0 and d % LANES == 0
    lc = ctx_len + seq
    t = batch * lc
    blocks_per_seq = lc // ROW_BLOCK
    ctx_blocks = ctx_len // ROW_BLOCK
    n_blocks = t // ROW_BLOCK
    n_mix_chunks = lc // MIX_CHUNK
    n_s5_chunks = lc // S5_CHUNK

    xs = jnp.concatenate([ctx, x], axis=1).reshape(t, d)

    mod_rows = ((batch + 1 + 7) // 8) * 8
    cvec = jnp.zeros((mod_rows, d), F32).at[:batch].set(c).at[batch].set(c_ctx)
    mod = _mod_call(cvec, w_mod, b_mod)
    blk = jnp.arange(n_blocks)
    mod_row = jnp.where(blk % blocks_per_seq < ctx_blocks, batch, blk // blocks_per_seq)

    tm_in = _largest_tile(t, 1024)
    tm_out = ROW_BLOCK
    tm_ffn = _largest_tile(t, 512)

    o_qkv = 0
    o_gz = o_qkv + 3 * GDN_DIM
    o_ga = o_gz + GDN_DIM
    o_gb = o_ga + 2 * GDN_HEADS
    o_sz = o_gb + 2 * GDN_HEADS
    o_xbc = o_sz + SSM_DIM
    o_sdt = o_xbc + SSM_CONV_DIM
    o_u = o_sdt + 2 * SSM_HEADS

    for i in range(depth):
        modtab = mod[i][mod_row].reshape(n_blocks, 1, 6 * d)
        wi = w_in[i]
        w_main = jnp.concatenate([wi[:, o_qkv:o_ga], wi[:, o_sz:o_sdt], wi[:, o_u:o_u + S5_DIM]],
                                 axis=1).astype(BF16)
        w_gate = jnp.concatenate([wi[:, o_ga:o_sz], wi[:, o_sdt:o_u]], axis=1)
        w_gate = jnp.pad(w_gate, ((0, 0), (0, GATE_DIM - w_gate.shape[1]))).astype(BF16)
        h, hg, hgt = _inproj_call(xs, norm_w[i, 0][None], modtab, w_main, w_gate, w_gate.T,
                                  tm_in, 512)

        qkv = _prep_call(h, COL_QKV, GDN_DIM, 3, gdn_conv[i], 1, 1, GDN_HEAD_DIM ** -0.5,
                         blocks_per_seq, ctx_blocks)
        hh = jnp.arange(GDN_HEADS)
        gsel = jnp.stack([GATE_A + hh, GATE_A + GDN_HEADS + hh, GATE_B + hh, GATE_B + GDN_HEADS + hh], axis=1)
        gcol = jnp.pad(jnp.transpose(hg[:, gsel], (1, 0, 2)), ((0, 0), (0, 0), (0, 4)))
        grow = jnp.pad(hgt[gsel], ((0, 0), (0, 4), (0, 0)))
        grow = jnp.transpose(grow.reshape(GDN_HEADS, 8, t // MIX_CHUNK, MIX_CHUNK), (0, 2, 1, 3))
        hp4 = jnp.concatenate([gdn_a_log[i], gdn_dt_bias[i]], axis=0).T
        hp = jnp.broadcast_to(jnp.pad(hp4, ((0, 0), (0, 4)))[:, :, None], (GDN_HEADS, 8, LANES))
        hpc = jnp.zeros((GDN_HEADS, 8, 8), F32)
        hpc = hpc.at[:, 0, 0:2].set(gdn_a_log[i].T).at[:, 1, 0:2].set(gdn_dt_bias[i].T)
        gdn_o = _gdn_call(qkv, h, gcol, grow, hp, hpc, gdn_norm[i][None], batch, lc,
                          ctx_len // MIX_CHUNK)

        xbc = _prep_call(h, COL_XBC, SSM_CONV_DIM, 1, ssm_conv[i], 0, 0, 1.0, blocks_per_seq, ctx_blocks)
        gh = jnp.arange(SSM_GROUP_HEADS)
        dsel = jnp.stack([jnp.concatenate([GATE_DT + g * SSM_GROUP_HEADS + gh,
                                           GATE_DT + SSM_HEADS + g * SSM_GROUP_HEADS + gh])
                          for g in range(SSM_GROUPS)])
        dcol = jnp.pad(jnp.transpose(hg[:, dsel], (1, 0, 2)), ((0, 0), (0, 0), (0, 4)))
        drow = jnp.pad(hgt[dsel], ((0, 0), (0, 4), (0, 0)))
        drow = jnp.transpose(drow.reshape(SSM_GROUPS, 16, t // MIX_CHUNK, MIX_CHUNK), (0, 2, 1, 3))
        sp = jnp.stack([ssm_a_log[i].reshape(2, SSM_GROUPS, SSM_GROUP_HEADS),
                        ssm_dt_bias[i].reshape(2, SSM_GROUPS, SSM_GROUP_HEADS)])
        sp = jnp.transpose(sp, (2, 0, 1, 3)).reshape(SSM_GROUPS, 2, 2 * SSM_GROUP_HEADS)
        prow = jnp.pad(sp, ((0, 0), (0, 6), (0, 4)))
        pcol = jnp.pad(jnp.transpose(sp, (0, 2, 1)), ((0, 0), (0, 4), (0, 6)))
        dskip = jnp.repeat(ssm_d[i], SSM_HEAD_DIM).reshape(SSM_GROUPS, 1, SSM_GROUP_DIM)
        ssd_o = _ssd_call(xbc, h, dcol, drow, prow, pcol, dskip,
                          ssm_norm[i].reshape(SSM_GROUPS, 1, SSM_GROUP_DIM), batch, lc,
                          ctx_len // MIX_CHUNK)

        w_so, w_intra, w_si, lam_c = _s5_operators(s5_lam_re[i], s5_lam_im[i], s5_log_step[i],
                                                   s5_b_re[i], s5_b_im[i], s5_c_re[i], s5_c_im[i])
        u = h[:, COL_U:COL_U + S5_DIM].astype(BF16)
        u_t = jnp.transpose(u.reshape(batch, n_s5_chunks, S5_CHUNK, S5_GROUPS, S5_GROUP), (0, 3, 1, 2, 4))
        u_t = u_t.reshape(batch, S5_GROUPS, n_s5_chunks, S5_CHUNK * S5_GROUP)
        y_t = _s5_call(u_t, w_so, w_intra, w_si, lam_c, ctx_len // S5_CHUNK, 16)
        y5 = jnp.transpose(y_t.reshape(batch, S5_GROUPS, n_s5_chunks, S5_CHUNK, S5_GROUP), (0, 2, 3, 1, 4))
        y5 = y5.reshape(t, S5_DIM)

        xs = _outproj_call(xs, gdn_o, ssd_o, y5, h, s5_d[i][None], s5_w_glu[i].astype(BF16),
                           s5_b_glu[i][None], w_out[i].astype(BF16), norm_w[i, 1][None], modtab, tm_out)
        xs = _ffn_call(xs, norm_w[i, 2][None], norm_w[i, 3][None], modtab,
                       w_ffn_in[i].astype(BF16), w_ffn_out[i].astype(BF16), tm_ffn, 512)

    return xs.reshape(batch, lc, d)[:, ctx_len:]
```
